```python
import jax, jax.numpy as jnp
from jax import lax
import numpy as np

D_MODEL = 1024
BATCH = 8
SEQ = 2048
DEPTH = 4

GRID_W = 64
CTX_LEN = 256
HEAD_DIM = 64
MIX_WIDTH = D_MODEL
A_HEADS = MIX_WIDTH // (2 * HEAD_DIM)
A_WIDTH = A_HEADS * HEAD_DIM
DECAY_LORA = 64
ICLR_LORA = 64
GATE_LORA = 128
B_HEADS = MIX_WIDTH // (2 * HEAD_DIM)
B_KV_HEADS = 2
WINDOW = 128
BLOCK = 128
C_HEADS = MIX_WIDTH // (2 * HEAD_DIM)
NA_KH = 8
NA_KW = 16
NA_QC = 16
NA_KC = 32
D_HEADS = MIX_WIDTH // (2 * HEAD_DIM)
D_KV_HEADS = 2
D_FF = 4 * D_MODEL
ROPE_BASE = 10000.0
NORM_EPS = 1e-6
GN_EPS = 64e-5
NEG_INF = -1e30
ATTN_SCALE = HEAD_DIM ** -0.5
N_EVEN = (DEPTH + 1) // 2
N_ODD = DEPTH // 2
A_IN = 3 * A_WIDTH + 2 * DECAY_LORA + 2 * ICLR_LORA + GATE_LORA
B_IN = (B_HEADS + 2 * B_KV_HEADS) * HEAD_DIM
C_IN = 3 * C_HEADS * HEAD_DIM
D_IN = (D_HEADS + 2 * D_KV_HEADS) * HEAD_DIM
EVEN_IN = A_IN + B_IN
ODD_IN = C_IN + D_IN
A_SPLITS = (A_WIDTH, 2 * A_WIDTH, 3 * A_WIDTH, 3 * A_WIDTH + 2 * DECAY_LORA,
            3 * A_WIDTH + 2 * DECAY_LORA + 2 * ICLR_LORA)

kernel_name = "hybrid_rwkv7_window_natten_gqa_diffusion_trunk"


def rms_norm(x, g):
    xf = x.astype(jnp.float32)
    y = xf * lax.rsqrt(jnp.mean(xf * xf, axis=-1, keepdims=True) + NORM_EPS)
    return (y * g.astype(jnp.float32)).astype(x.dtype)


def modulate(h, gain, shift, scale):
    return rms_norm(h, gain) * (1 + scale) + shift


def sqrelu_mlp(u, w1, w2):
    return jnp.square(jax.nn.relu(u @ w1)) @ w2


def axial_angles(n_tok):
    t = jnp.arange(n_tok, dtype=jnp.int32)
    n_freq = HEAD_DIM // 4
    inv_freq = ROPE_BASE ** (-jnp.arange(n_freq, dtype=jnp.float32) / n_freq)
    row = (t // GRID_W).astype(jnp.float32)[:, None] * inv_freq
    col = (t % GRID_W).astype(jnp.float32)[:, None] * inv_freq
    return row, col


def _rotate(x, ang):
    f = ang.shape[-1]
    cos = jnp.cos(ang)[None, :, None, :]
    sin = jnp.sin(ang)[None, :, None, :]
    x1 = x[..., :f].astype(jnp.float32)
    x2 = x[..., f:].astype(jnp.float32)
    return jnp.concatenate([x1 * cos - x2 * sin, x1 * sin + x2 * cos], axis=-1).astype(x.dtype)


def axial_rope(x, ang):
    half = HEAD_DIM // 2
    return jnp.concatenate([_rotate(x[..., :half], ang[0]), _rotate(x[..., half:], ang[1])], axis=-1)


def split_qkv(p, hq, hkv):
    bn, s, _ = p.shape
    q, k, v = jnp.split(p, [hq * HEAD_DIM, (hq + hkv) * HEAD_DIM], axis=-1)
    return (q.reshape(bn, s, hq, HEAD_DIM), k.reshape(bn, s, hkv, HEAD_DIM), v.reshape(bn, s, hkv, HEAD_DIM))


def softmax_with_sink(s, sink):
    sk = jnp.broadcast_to(sink.astype(jnp.float32)[:, :, None, None], s.shape[:-1] + (1,))
    return jax.nn.softmax(jnp.concatenate([s, sk], axis=-1), axis=-1)[..., :-1]


def context_attention(q, k, v, sink=None):
    bn, l, h, d = q.shape
    hkv = k.shape[2]
    qg = q.reshape(bn, l, hkv, h // hkv, d)
    s = jnp.einsum('blkgd,bckd->bkglc', qg, k).astype(jnp.float32) * ATTN_SCALE
    p = jax.nn.softmax(s, axis=-1) if sink is None else softmax_with_sink(s, sink)
    o = jnp.einsum('bkglc,bckd->blkgd', p.astype(v.dtype), v)
    return o.reshape(bn, l, h * d)


def centred_shift(p, mu_prev, mu_next):
    prev = jnp.pad(p, ((0, 0), (1, 0), (0, 0)))[:, :-1]
    nxt = jnp.pad(p, ((0, 0), (0, 1), (0, 0)))[:, 1:]
    return p + mu_prev * (prev - p) + mu_next * (nxt - p)


def rwkv7_prepare(pa, prm):
    p = centred_shift(pa, prm['mu_prev'], prm['mu_next']).astype(jnp.float32)
    bn, s, _ = p.shape
    r, k, v, wl, al, gl = jnp.split(p, A_SPLITS, axis=-1)
    wl = wl.reshape(bn, s, 2, DECAY_LORA)
    al = al.reshape(bn, s, 2, ICLR_LORA)
    w_raw = prm['w0'] + jnp.einsum('bsdl,dlc->bsdc', jnp.tanh(wl), prm['w2'])
    decay = jnp.exp(-jnp.exp(-jax.nn.softplus(-w_raw) - 0.5))
    iclr = jax.nn.sigmoid(prm['a0'] + jnp.einsum('bsdl,dlc->bsdc', al, prm['a2']))
    g = jax.nn.sigmoid(gl) @ prm['g2']
    kk = (k * prm['k_k']).reshape(bn, s, A_HEADS, HEAD_DIM)
    kk = kk / jnp.maximum(jnp.sqrt(jnp.sum(kk * kk, axis=-1, keepdims=True)), 1e-12)
    k_dir = k[:, :, None, :] * (1 + (iclr - 1) * prm['k_a'])
    heads = lambda t: t.reshape(t.shape[:-1] + (A_HEADS, HEAD_DIM))
    return dict(r=heads(r), v=heads(v), g=g, kk=kk, k=heads(k_dir), w=heads(decay), a=heads(iclr))


def rwkv7_scan(f, d, s0, reverse):
    def step(state, inp):
        r_t, w_t, k_t, v_t, kk_t, a_t = inp
        sa = jnp.einsum('bhvk,bhk->bhv', state, -kk_t)
        state = (state * w_t[:, :, None, :] + sa[..., None] * (kk_t * a_t)[:, :, None, :]
                 + v_t[..., None] * k_t[:, :, None, :])
        return state, jnp.einsum('bhvk,bhk->bhv', state, r_t)
    seq = (f['r'], f['w'][:, :, d], f['k'][:, :, d], f['v'], f['kk'], f['a'][:, :, d])
    xs = tuple(jnp.moveaxis(t, 1, 0) for t in seq)
    s_last, ys = lax.scan(step, s0, xs, reverse=reverse)
    return s_last, jnp.moveaxis(ys, 0, 1)


def rwkv7_output(f, y, prm, dtype):
    bn, s = y.shape[:2]
    mu = jnp.mean(y, axis=-1, keepdims=True)
    var = jnp.mean(jnp.square(y - mu), axis=-1, keepdims=True)
    yn = ((y - mu) * lax.rsqrt(var + GN_EPS)).reshape(bn, s, A_WIDTH) * prm['gn_w'] + prm['gn_b']
    bonus = jnp.sum(f['r'][:, :, None] * f['k'] * prm['r_k'], axis=(2, 4))
    yn = yn + (bonus[..., None] * f['v']).reshape(bn, s, A_WIDTH)
    return (yn * f['g']).astype(dtype)


def rwkv7_mixer(pa_ctx, pa_lat, prm, need_ctx):
    fc = rwkv7_prepare(pa_ctx, prm)
    fl = rwkv7_prepare(pa_lat, prm)
    zero = jnp.zeros((pa_lat.shape[0], A_HEADS, HEAD_DIM, HEAD_DIM), jnp.float32)
    s_cf, yc_f = rwkv7_scan(fc, 0, zero, False)
    _, yl_f = rwkv7_scan(fl, 0, s_cf, False)
    s_cb, yc_b = rwkv7_scan(fc, 1, zero, True)
    _, yl_b = rwkv7_scan(fl, 1, s_cb, True)
    out_lat = rwkv7_output(fl, yl_f + yl_b, prm, pa_lat.dtype)
    out_ctx = rwkv7_output(fc, yc_f + yc_b, prm, pa_ctx.dtype) if need_ctx else None
    return out_lat, out_ctx


def banded_window_attention(q, k, v, k_ctx, v_ctx, sink):
    bn, s, h, d = q.shape
    hkv = k.shape[2]
    nb = s // BLOCK
    qb = q.reshape(bn, nb, BLOCK, hkv, h // hkv, d)

    def band(t):
        tp = jnp.pad(t, ((0, 0), (BLOCK, BLOCK), (0, 0), (0, 0)))
        return jnp.concatenate([tp[:, j * BLOCK: j * BLOCK + s].reshape(bn, nb, BLOCK, hkv, d)
                                for j in range(3)], axis=2)
    kb, vb = band(k), band(v)
    qi = np.arange(BLOCK)[:, None]
    kj = np.arange(3 * BLOCK)[None, :]
    kpos = (np.arange(nb) * BLOCK)[:, None, None] - BLOCK + kj[None]
    ok = (np.abs(kj - BLOCK - qi) <= WINDOW)[None] & (kpos >= 0) & (kpos < s)
    s_lat = jnp.einsum('bnqkgd,bnjkd->bnkgqj', qb, kb).astype(jnp.float32) * ATTN_SCALE
    s_lat = jnp.where(ok[None, :, None, None], s_lat, NEG_INF)
    s_ctx = jnp.einsum('bnqkgd,bckd->bnkgqc', qb, k_ctx).astype(jnp.float32) * ATTN_SCALE
    p = softmax_with_sink(jnp.concatenate([s_lat, s_ctx], axis=-1), sink)
    p_lat, p_ctx = p[..., :3 * BLOCK].astype(v.dtype), p[..., 3 * BLOCK:].astype(v.dtype)
    o = (jnp.einsum('bnkgqj,bnjkd->bnqkgd', p_lat, vb)
         + jnp.einsum('bnkgqc,bckd->bnqkgd', p_ctx, v_ctx))
    return o.reshape(bn, s, h * d)


def neighbourhood_tables(rows):
    kh = min(NA_KH, rows)
    r = np.arange(rows)
    row_idx = np.clip(r - kh // 2, 0, rows - kh)[:, None] + np.arange(kh)[None, :]
    dr = row_idx - r[:, None] + NA_KH - 1
    ncb = GRID_W // NA_QC
    q_col = np.arange(ncb)[:, None] * NA_QC + np.arange(NA_QC)[None, :]
    key_col = (np.clip(np.arange(ncb) * NA_QC - NA_KW // 2, 0, GRID_W - NA_KC)[:, None]
               + np.arange(NA_KC)[None, :])
    win_start = np.clip(q_col - NA_KW // 2, 0, GRID_W - NA_KW)[:, :, None]
    kc = key_col[:, None, :]
    col_ok = (kc >= win_start) & (kc < win_start + NA_KW)
    dc = np.clip(kc - q_col[:, :, None] + NA_KW - 1, 0, 2 * NA_KW - 2)
    return row_idx, dr, key_col, col_ok, dc


def neighbourhood_attention(q, k, v, k_ctx, v_ctx, rpb):
    bn, s, h, d = q.shape
    rows = s // GRID_W
    ncb = GRID_W // NA_QC
    row_idx, dr, key_col, col_ok, dc = neighbourhood_tables(rows)
    kh = row_idx.shape[1]

    def gather(t):
        return t.reshape(bn, rows, GRID_W, h, d)[:, row_idx][:, :, :, key_col]
    kg, vg = gather(k), gather(v)
    qg = q.reshape(bn, rows, ncb, NA_QC, h, d)
    s_lat = jnp.einsum('brmqhd,brtmjhd->brmhqtj', qg, kg).astype(jnp.float32) * ATTN_SCALE
    bias = rpb[:, dr[:, None, None, :, None], dc[None, :, :, None, :]]
    s_lat = s_lat + bias.transpose(1, 2, 0, 3, 4, 5).astype(jnp.float32)
    s_lat = jnp.where(col_ok[:, None, :, None, :], s_lat, NEG_INF)
    s_lat = s_lat.reshape(s_lat.shape[:-2] + (kh * NA_KC,))
    s_ctx = jnp.einsum('brmqhd,bchd->brmhqc', qg, k_ctx).astype(jnp.float32) * ATTN_SCALE
    p = jax.nn.softmax(jnp.concatenate([s_lat, s_ctx], axis=-1), axis=-1)
    p_lat = p[..., :kh * NA_KC].reshape(p.shape[:-1] + (kh, NA_KC)).astype(v.dtype)
    p_ctx = p[..., kh * NA_KC:].astype(v.dtype)
    o = (jnp.einsum('brmhqtj,brtmjhd->brmqhd', p_lat, vg)
         + jnp.einsum('brmhqc,bchd->brmqhd', p_ctx, v_ctx))
    return o.reshape(bn, s, h * d)


def global_block_attention(q, k, v, k_ctx, v_ctx):
    bn, s, h, d = q.shape
    hkv = k.shape[2]
    nb = s // BLOCK
    k_all = jnp.concatenate([k_ctx, k], axis=1)
    v_all = jnp.concatenate([v_ctx, v], axis=1)
    qb = q.reshape(bn, nb, BLOCK, hkv, h // hkv, d).transpose(1, 0, 2, 3, 4, 5)

    def one_block(qblk):
        sc = jnp.einsum('bqkgd,bskd->bkgqs', qblk, k_all).astype(jnp.float32) * ATTN_SCALE
        p = jax.nn.softmax(sc, axis=-1).astype(v_all.dtype)
        return jnp.einsum('bkgqs,bskd->bqkgd', p, v_all)
    o = lax.map(one_block, qb)
    return o.transpose(1, 0, 2, 3, 4, 5).reshape(bn, s, h * d)


def even_mixer(u_ctx, u_lat, w_in, w_out, a_prm, sink, rope_ang, need_ctx):
    p_ctx = u_ctx @ w_in
    p_lat = u_lat @ w_in
    a_lat, a_ctx = rwkv7_mixer(p_ctx[..., :A_IN], p_lat[..., :A_IN], a_prm, need_ctx)
    qc, kc, vc = split_qkv(p_ctx[..., A_IN:], B_HEADS, B_KV_HEADS)
    ql, kl, vl = split_qkv(p_lat[..., A_IN:], B_HEADS, B_KV_HEADS)
    ql, kl = axial_rope(ql, rope_ang), axial_rope(kl, rope_ang)
    sink_g = sink.reshape(B_KV_HEADS, B_HEADS // B_KV_HEADS)
    b_lat = banded_window_attention(ql, kl, vl, kc, vc, sink_g)
    out_lat = jnp.concatenate([a_lat, b_lat], axis=-1) @ w_out
    if not need_ctx:
        return out_lat, None
    b_ctx = context_attention(qc, kc, vc, sink_g)
    return out_lat, jnp.concatenate([a_ctx, b_ctx], axis=-1) @ w_out


def odd_mixer(u_ctx, u_lat, w_in, w_out, rpb, q_gain, k_gain, rope_ang, need_ctx):
    p_ctx = u_ctx @ w_in
    p_lat = u_lat @ w_in
    cq_c, ck_c, cv_c = split_qkv(p_ctx[..., :C_IN], C_HEADS, C_HEADS)
    cq_l, ck_l, cv_l = split_qkv(p_lat[..., :C_IN], C_HEADS, C_HEADS)
    c_lat = neighbourhood_attention(cq_l, ck_l, cv_l, ck_c, cv_c, rpb)
    dq_c, dk_c, dv_c = split_qkv(p_ctx[..., C_IN:], D_HEADS, D_KV_HEADS)
    dq_l, dk_l, dv_l = split_qkv(p_lat[..., C_IN:], D_HEADS, D_KV_HEADS)
    dq_c, dk_c = rms_norm(dq_c, q_gain), rms_norm(dk_c, k_gain)
    dq_l = axial_rope(rms_norm(dq_l, q_gain), rope_ang)
    dk_l = axial_rope(rms_norm(dk_l, k_gain), rope_ang)
    d_lat = global_block_attention(dq_l, dk_l, dv_l, dk_c, dv_c)
    out_lat = jnp.concatenate([c_lat, d_lat], axis=-1) @ w_out
    if not need_ctx:
        return out_lat, None
    c_ctx_o = context_attention(cq_c, ck_c, cv_c)
    d_ctx_o = context_attention(dq_c, dk_c, dv_c)
    return out_lat, jnp.concatenate([c_ctx_o, d_ctx_o], axis=-1) @ w_out


def setup_inputs(seed: int = 0) -> dict:
    key = jax.random.key(seed)
    ks = iter(jax.random.split(key, 40))
    nrm = lambda shape, scale: scale * jax.random.normal(next(ks), shape, jnp.float32)
    unif = lambda shape, lo, hi: jax.random.uniform(next(ks), shape, jnp.float32, lo, hi)
    D = D_MODEL
    return {
        "x": nrm((BATCH, SEQ, D), 1.0),
        "c": nrm((BATCH, D), 1.0),
        "ctx": nrm((BATCH, CTX_LEN, D), 1.0),
        "c_ctx": nrm((D,), 1.0),
        "w_ada": nrm((DEPTH, D, 6 * D), 0.5 * D ** -0.5),
        "b_ada": nrm((DEPTH, 6 * D), 0.02),
        "g_pre_mix": 1.0 + nrm((DEPTH, D), 0.05),
        "g_post_mix": 1.0 + nrm((DEPTH, D), 0.05),
        "g_pre_ff": 1.0 + nrm((DEPTH, D), 0.05),
        "g_post_ff": 1.0 + nrm((DEPTH, D), 0.05),
        "w_in_even": nrm((N_EVEN, D, EVEN_IN), D ** -0.5),
        "w_in_odd": nrm((N_ODD, D, ODD_IN), D ** -0.5),
        "w_out": nrm((DEPTH, MIX_WIDTH, D), MIX_WIDTH ** -0.5),
        "w_ff1": nrm((DEPTH, D, D_FF), D ** -0.5),
        "w_ff2": nrm((DEPTH, D_FF, D), D_FF ** -0.5),
        "a_mu_prev": unif((N_EVEN, A_IN), 0.05, 0.45),
        "a_mu_next": unif((N_EVEN, A_IN), 0.05, 0.45),
        "a_w0": unif((N_EVEN, 2, A_WIDTH), -5.0, -0.5),
        "a_w2": nrm((N_EVEN, 2, DECAY_LORA, A_WIDTH), 0.1 * DECAY_LORA ** -0.5),
        "a_a0": nrm((N_EVEN, 2, A_WIDTH), 0.1),
        "a_a2": nrm((N_EVEN, 2, ICLR_LORA, A_WIDTH), 0.5 * ICLR_LORA ** -0.5),
        "a_g2": nrm((N_EVEN, GATE_LORA, A_WIDTH), GATE_LORA ** -0.5),
        "a_k_k": 0.85 + nrm((N_EVEN, A_WIDTH), 0.05),
        "a_k_a": 1.0 + nrm((N_EVEN, A_WIDTH), 0.05),
        "a_r_k": nrm((N_EVEN, A_HEADS, HEAD_DIM), 0.1),
        "a_gn_w": 1.0 + nrm((N_EVEN, A_WIDTH), 0.05),
        "a_gn_b": nrm((N_EVEN, A_WIDTH), 0.02),
        "b_sink": nrm((N_EVEN, B_HEADS), 0.5),
        "c_rpb": nrm((N_ODD, C_HEADS, 2 * NA_KH - 1, 2 * NA_KW - 1), 0.5),
        "d_q_gain": 1.0 + nrm((N_ODD, HEAD_DIM), 0.05),
        "d_k_gain": 1.0 + nrm((N_ODD, HEAD_DIM), 0.05),
    }


def reference(x, c, ctx, c_ctx, w_ada, b_ada, g_pre_mix, g_post_mix, g_pre_ff, g_post_ff,
              w_in_even, w_in_odd, w_out, w_ff1, w_ff2,
              a_mu_prev, a_mu_next, a_w0, a_w2, a_a0, a_a2, a_g2, a_k_k, a_k_a, a_r_k, a_gn_w, a_gn_b,
              b_sink, c_rpb, d_q_gain, d_k_gain):
    rope_ang = axial_angles(x.shape[1])
    s_lat = jax.nn.silu(c)[:, None, :]
    s_ctx = jax.nn.silu(c_ctx)[None, None, :]
    h_lat, h_ctx = x, ctx
    for i in range(DEPTH):
        need_ctx = i < DEPTH - 1
        j = i // 2
        ml = jnp.split(s_lat @ w_ada[i] + b_ada[i], 6, axis=-1)
        mc = jnp.split(s_ctx @ w_ada[i] + b_ada[i], 6, axis=-1)
        u_lat = modulate(h_lat, g_pre_mix[i], ml[0], ml[1])
        u_ctx = modulate(h_ctx, g_pre_mix[i], mc[0], mc[1])
        if i % 2 == 0:
            a_prm = dict(mu_prev=a_mu_prev[j], mu_next=a_mu_next[j], w0=a_w0[j], w2=a_w2[j],
                         a0=a_a0[j], a2=a_a2[j], g2=a_g2[j], k_k=a_k_k[j], k_a=a_k_a[j],
                         r_k=a_r_k[j], gn_w=a_gn_w[j], gn_b=a_gn_b[j])
            o_lat, o_ctx = even_mixer(u_ctx, u_lat, w_in_even[j], w_out[i], a_prm, b_sink[j], rope_ang, need_ctx)
        else:
            o_lat, o_ctx = odd_mixer(u_ctx, u_lat, w_in_odd[j], w_out[i], c_rpb[j], d_q_gain[j], d_k_gain[j],
                                     rope_ang, need_ctx)
        h_lat = h_lat + ml[2] * rms_norm(o_lat, g_post_mix[i])
        f_lat = sqrelu_mlp(modulate(h_lat, g_pre_ff[i], ml[3], ml[4]), w_ff1[i], w_ff2[i])
        h_lat = h_lat + ml[5] * rms_norm(f_lat, g_post_ff[i])
        if need_ctx:
            h_ctx = h_ctx + mc[2] * rms_norm(o_ctx, g_post_mix[i])
            f_ctx = sqrelu_mlp(modulate(h_ctx, g_pre_ff[i], mc[3], mc[4]), w_ff1[i], w_ff2[i])
            h_ctx = h_ctx + mc[5] * rms_norm(f_ctx, g_post_ff[i])
    return h_lat
```

```python
import functools

import numpy as np
import jax
import jax.numpy as jnp
from jax import lax
from jax.experimental import pallas as pl
from jax.experimental.pallas import tpu as pltpu

F32 = jnp.float32
BF16 = jnp.bfloat16

HEAD_DIM = 64
LANES = 128
N_PAIRS = 4
GROUP_W = N_PAIRS * LANES
KV_W = LANES
DECAY_LORA = 64
GATE_LORA = 128
WINDOW = 128
BLOCK = 128
GRID_W = 64
NA_KH = 8
NA_KW = 16
ROPE_BASE = 10000.0
NORM_EPS = 1e-6
GN_EPS = 64e-5
NEG_INF = -1e30
ATTN_SCALE = HEAD_DIM ** -0.5
CHUNK = 64
RW_TILE = 256
VMEM_LIMIT = 56 * 1024 * 1024


def _cparams(sem):
    return pltpu.CompilerParams(dimension_semantics=sem, vmem_limit_bytes=VMEM_LIMIT)


def _dot(a, b):
    return jnp.dot(a.astype(BF16), b.astype(BF16), preferred_element_type=F32)


def _dot_nt(a, b):
    return lax.dot_general(a.astype(BF16), b.astype(BF16), (((1,), (1,)), ((), ())),
                           preferred_element_type=F32)


def _dot_tn(a, b):
    return lax.dot_general(a.astype(BF16), b.astype(BF16), (((0,), (0,)), ((), ())),
                           preferred_element_type=F32)


def _dot_f32(a, b):
    return jnp.dot(a, b, preferred_element_type=F32, precision=lax.Precision.HIGHEST)


def _lane_half(shape):
    return lax.broadcasted_iota(jnp.int32, shape, len(shape) - 1) % LANES < HEAD_DIM


def _head_ones():
    r = lax.broadcasted_iota(jnp.int32, (LANES, LANES), 0) // HEAD_DIM
    c = lax.broadcasted_iota(jnp.int32, (LANES, LANES), 1) // HEAD_DIM
    return (r == c).astype(F32)


def _embed(x):
    first = _lane_half(x.shape)
    zero = jnp.zeros_like(x)
    return jnp.concatenate([jnp.where(first, x, zero), jnp.where(first, zero, x)], axis=0)


def _fold(x):
    n = x.shape[0] // 2
    return x[:n] + x[n:]


def _sigmoid(x):
    return 1.0 / (1.0 + jnp.exp(-x))


def _rms_rows(x, gain):
    return x * lax.rsqrt(jnp.mean(x * x, axis=-1, keepdims=True) + NORM_EPS) * gain


def _rope(x, cos, sin):
    lane = lax.broadcasted_iota(jnp.int32, x.shape, 1)
    swapped = jnp.where(lane % 32 < 16, pltpu.roll(x, LANES - 16, 1), pltpu.roll(x, 16, 1))
    return x * cos + swapped * sin


def _head_rms(x, gain):
    ms = _dot_f32(x * x, _head_ones()) * (1.0 / HEAD_DIM)
    return x * lax.rsqrt(ms + NORM_EPS) * gain


def _to_half(x, src, dst):
    if src != dst:
        x = pltpu.roll(x, HEAD_DIM, 1)
    first = _lane_half(x.shape)
    keep = first if dst == 0 else jnp.logical_not(first)
    return jnp.where(keep, x, jnp.zeros_like(x))


def _softmax_pv(s, v, sink=None):
    m = jnp.max(s, axis=-1, keepdims=True)
    if sink is not None:
        m = jnp.maximum(m, sink)
    p = jnp.exp(s - m)
    den = jnp.sum(p, axis=-1, keepdims=True)
    if sink is not None:
        den = den + jnp.exp(sink - m)
    return _dot(p, v) / den


def _merge_heads(o0, src0, o1, src1):
    a = o0 if src0 == 0 else pltpu.roll(o0, HEAD_DIM, 1)
    b = o1 if src1 == 1 else pltpu.roll(o1, HEAD_DIM, 1)
    return jnp.where(_lane_half(a.shape), a, b)


def _mod_body(c_ref, w_ref, b_ref, o_ref):
    c = c_ref[...]
    s = c * _sigmoid(c)
    o_ref[...] = _dot_f32(s, w_ref[...]) + b_ref[...]


def _modulation(cvec, w_ada, b_ada):
    depth, d, _ = w_ada.shape
    out = pl.pallas_call(
        _mod_body,
        grid=(depth, 6),
        in_specs=[pl.BlockSpec((16, d), lambda l, n: (0, 0)),
                  pl.BlockSpec((None, d, d), lambda l, n: (l, 0, n)),
                  pl.BlockSpec((None, None, 1, d), lambda l, n: (l, n, 0, 0))],
        out_specs=pl.BlockSpec((None, None, 16, d), lambda l, n: (l, n, 0, 0)),
        out_shape=jax.ShapeDtypeStruct((depth, 6, 16, d), F32),
        compiler_params=_cparams(("parallel", "parallel")),
        name="adaln_modulation",
    )(cvec, w_ada, b_ada.reshape(depth, 6, 1, d))
    return out.transpose(0, 2, 1, 3)


def _mod_row(tile, tm, n_ctx_rows, ls, nb):
    start = tile * tm
    return jnp.where(start < n_ctx_rows, nb, (start - n_ctx_rows) // ls)


def _inproj_body(x_ref, m_ref, g_ref, w_ref, *o_refs, pieces):
    u = _rms_rows(x_ref[...], g_ref[...]) * (1.0 + m_ref[1:2, :]) + m_ref[0:1, :]
    u = u.astype(BF16)
    for (src, span, dests) in pieces:
        val = jnp.dot(u, w_ref[:, src:src + span], preferred_element_type=F32)
        for (oi, lead, dst, off, width) in dests:
            if lead is None:
                o_refs[oi][:, dst:dst + width] = val[:, off:off + width]
            else:
                o_refs[oi][lead, :, dst:dst + width] = val[:, off:off + width]


def _in_proj(h, mods_l, gain, w, out_defs, pieces, *, tm, n_ctx_rows, ls, nb):
    t, d = h.shape
    n_in = w.shape[1]
    out_shapes, out_specs = [], []
    for shape in out_defs:
        out_shapes.append(jax.ShapeDtypeStruct(shape, F32))
        if len(shape) == 3:
            out_specs.append(pl.BlockSpec((shape[0], tm, shape[2]), lambda i: (0, i, 0)))
        else:
            out_specs.append(pl.BlockSpec((tm, shape[1]), lambda i: (i, 0)))
    row = functools.partial(_mod_row, tm=tm, n_ctx_rows=n_ctx_rows, ls=ls, nb=nb)
    return pl.pallas_call(
        functools.partial(_inproj_body, pieces=pieces),
        grid=(t // tm,),
        in_specs=[pl.BlockSpec((tm, d), lambda i: (i, 0)),
                  pl.BlockSpec((None, 6, d), lambda i: (row(i), 0, 0)),
                  pl.BlockSpec((1, d), lambda i: (0, 0)),
                  pl.BlockSpec((d, n_in), lambda i: (0, 0))],
        out_specs=out_specs,
        out_shape=out_shapes,
        compiler_params=_cparams(("parallel",)),
        name="modulate_in_proj",
    )(h, mods_l, gain.reshape(1, d), w)


def _outproj_body(a_ref, b_ref, h_ref, m_ref, g_ref, w_ref, o_ref):
    o = (jnp.dot(a_ref[...], w_ref[0:GROUP_W, :], preferred_element_type=F32)
         + jnp.dot(b_ref[...], w_ref[GROUP_W:2 * GROUP_W, :], preferred_element_type=F32))
    o_ref[...] = h_ref[...] + m_ref[2:3, :] * _rms_rows(o, g_ref[...])


def _out_proj(mix_a, mix_b, h, mods_l, gain, w, *, tm, n_ctx_rows, ls, nb):
    t, d = h.shape
    row = functools.partial(_mod_row, tm=tm, n_ctx_rows=n_ctx_rows, ls=ls, nb=nb)
    return pl.pallas_call(
        _outproj_body,
        grid=(t // tm,),
        in_specs=[pl.BlockSpec((tm, GROUP_W), lambda i: (i, 0)),
                  pl.BlockSpec((tm, GROUP_W), lambda i: (i, 0)),
                  pl.BlockSpec((tm, d), lambda i: (i, 0)),
                  pl.BlockSpec((None, 6, d), lambda i: (row(i), 0, 0)),
                  pl.BlockSpec((1, d), lambda i: (0, 0)),
                  pl.BlockSpec((2 * GROUP_W, d), lambda i: (0, 0))],
        out_specs=pl.BlockSpec((tm, d), lambda i: (i, 0)),
        out_shape=jax.ShapeDtypeStruct((t, d), F32),
        compiler_params=_cparams(("parallel",)),
        name="out_proj_gate_residual",
    )(mix_a, mix_b, h, mods_l, gain.reshape(1, d), w)


def _mlp_body(x_ref, m_ref, g1_ref, g2_ref, w1_ref, w2_ref, o_ref, u_scr, acc_scr):
    k = pl.program_id(1)

    @pl.when(k == 0)
    def _():
        u = _rms_rows(x_ref[...], g1_ref[...]) * (1.0 + m_ref[4:5, :]) + m_ref[3:4, :]
        u_scr[...] = u.astype(BF16)
        acc_scr[...] = jnp.zeros_like(acc_scr)

    a = jnp.dot(u_scr[...], w1_ref[...], preferred_element_type=F32)
    a = jnp.square(jnp.maximum(a, 0.0))
    acc_scr[...] += jnp.dot(a.astype(BF16), w2_ref[...], preferred_element_type=F32)

    @pl.when(k == pl.num_programs(1) - 1)
    def _():
        o_ref[...] = x_ref[...] + m_ref[5:6, :] * _rms_rows(acc_scr[...], g2_ref[...])


def _mlp(h, mods_l, g_pre, g_post, w1, w2, *, tm, fc, n_ctx_rows, ls, nb, first_row):
    t, d = h.shape
    dff = w1.shape[1]
    off = first_row // tm
    row = functools.partial(_mod_row, tm=tm, n_ctx_rows=n_ctx_rows, ls=ls, nb=nb)
    return pl.pallas_call(
        _mlp_body,
        grid=((t - first_row) // tm, dff // fc),
        in_specs=[pl.BlockSpec((tm, d), lambda i, k: (i + off, 0)),
                  pl.BlockSpec((None, 6, d), lambda i, k: (row(i + off), 0, 0)),
                  pl.BlockSpec((1, d), lambda i, k: (0, 0)),
                  pl.BlockSpec((1, d), lambda i, k: (0, 0)),
                  pl.BlockSpec((d, fc), lambda i, k: (0, k)),
                  pl.BlockSpec((fc, d), lambda i, k: (k, 0))],
        out_specs=pl.BlockSpec((tm, d), lambda i, k: (i, 0)),
        out_shape=jax.ShapeDtypeStruct((t - first_row, d), F32),
        scratch_shapes=[pltpu.VMEM((tm, d), BF16), pltpu.VMEM((tm, d), F32)],
        compiler_params=_cparams(("parallel", "arbitrary")),
        name="sqrelu_mlp_gate_residual",
    )(h, mods_l, g_pre.reshape(1, d), g_post.reshape(1, d), w1, w2)


def _rwkv_unit(r, v, kd, alpha, beta, logw, reverse):
    n = r.shape[0]
    ri = lax.broadcasted_iota(jnp.int32, (n, n), 0)
    ci = lax.broadcasted_iota(jnp.int32, (n, n), 1)
    incl = (ci >= ri) if reverse else (ci <= ri)
    c = _dot_f32(incl.astype(F32), logw)
    c_all = jnp.sum(logw, axis=0, keepdims=True)
    e_pos = jnp.exp(c)
    e_neg = jnp.exp(-c)
    e_end = jnp.exp(c_all - c)
    rt = _embed(r * e_pos)
    at = _embed(alpha * jnp.exp(c - logw))
    kt = _embed(kd * e_neg)
    bt = _embed(beta * e_neg)
    kh = _embed(kd * e_end)
    bh = _embed(beta * e_end)
    vb = _embed(v)
    n2 = 2 * n
    a_all = _dot_nt(jnp.concatenate([rt, at], axis=0), jnp.concatenate([kt, bt], axis=0))
    r2 = lax.broadcasted_iota(jnp.int32, (n2, n2), 0) % n
    c2 = lax.broadcasted_iota(jnp.int32, (n2, n2), 1) % n
    incl2 = (c2 >= r2) if reverse else (c2 <= r2)
    strict2 = (c2 > r2) if reverse else (c2 < r2)
    zero = jnp.zeros((n2, n2), F32)
    a_qk = jnp.where(incl2, a_all[:n2, :n2], zero)
    a_qb = jnp.where(incl2, a_all[:n2, n2:], zero)
    a_ak = jnp.where(strict2, a_all[n2:, :n2], zero)
    a_ab = jnp.where(strict2, a_all[n2:, n2:], zero)
    eye = (lax.broadcasted_iota(jnp.int32, (n2, n2), 0)
           == lax.broadcasted_iota(jnp.int32, (n2, n2), 1))
    rr = lax.broadcasted_iota(jnp.int32, (n2, n2), 0)
    cc = lax.broadcasted_iota(jnp.int32, (n2, n2), 1)
    base = 8
    apow = jnp.where(rr // base == cc // base, a_ab, zero)
    inv = jnp.where(eye, 1.0, 0.0) + apow
    for _ in range(2):
        apow = _dot(apow, apow)
        inv = inv + _dot(inv, apow)
    size = base
    while size < n:
        off = jnp.where((rr // (2 * size) == cc // (2 * size)) & (rr // size != cc // size), a_ab, zero)
        inv = inv + _dot(inv, _dot(off, inv))
        size *= 2
    wu = _dot(inv, jnp.concatenate([at, _dot(a_ak, vb)], axis=1))
    w = wu[:, :LANES]
    u0 = wu[:, LANES:]
    u0v = jnp.concatenate([u0, vb], axis=0)
    m = jnp.where(eye, jnp.exp(c_all), 0.0) + _dot_tn(bh, w)
    nn = _dot_tn(jnp.concatenate([bh, kh], axis=0), u0v)
    rm = rt + _dot(a_qb, w)
    y0 = _dot(jnp.concatenate([a_qb, a_qk], axis=1), u0v)
    return _fold(rm), _fold(y0), _fold(m), _fold(nn)


def _shifted(x, prev_row, next_row, mu_prev, mu_next):
    rows = x.shape[0]
    ridx = lax.broadcasted_iota(jnp.int32, x.shape, 0)
    prev = jnp.where(ridx == 0, prev_row, pltpu.roll(x, 1, 0))
    nxt = jnp.where(ridx == rows - 1, next_row, pltpu.roll(x, rows - 1, 0))
    return x + mu_prev * (prev - x) + mu_next * (nxt - x)


def _rwkv_pre_body(rkv_ref, rkv_p_ref, rkv_n_ref, lo_ref, lo_p_ref, lo_n_ref,
                   mu_rkv_ref, mu_lo_ref, vec_ref, w2_ref, a2_ref, g2_ref,
                   rm_ref, y0_ref, m_ref, n_ref, bv_ref, g_ref,
                   r_s, v_s, kd_s, al_s, be_s, lw_s, *, lat_tiles, ctx_tiles, n_ctx_tiles):
    i = pl.program_id(0)
    lat = i >= n_ctx_tiles
    pos = jnp.where(lat, (i - n_ctx_tiles) % lat_tiles, i % ctx_tiles)
    has_prev = (pos != 0).astype(F32)
    has_next = (pos != jnp.where(lat, lat_tiles, ctx_tiles) - 1).astype(F32)
    rkv = _shifted(rkv_ref[...], rkv_p_ref[7:8, :] * has_prev, rkv_n_ref[0:1, :] * has_next,
                   mu_rkv_ref[0:1, :], mu_rkv_ref[1:2, :])
    lo = _shifted(lo_ref[...], lo_p_ref[7:8, :] * has_prev, lo_n_ref[0:1, :] * has_next,
                  mu_lo_ref[0:1, :], mu_lo_ref[1:2, :])
    r = rkv[:, 0:LANES]
    k = rkv[:, LANES:2 * LANES]
    v = rkv[:, 2 * LANES:3 * LANES]
    wl = jnp.tanh(lo[:, 0:LANES])
    al = lo[:, LANES:2 * LANES]
    gl = _sigmoid(lo[:, 2 * LANES:3 * LANES])
    ones = _head_ones()
    kx = k * vec_ref[4:5, :]
    kk = kx / jnp.maximum(jnp.sqrt(_dot_f32(kx * kx, ones)), 1e-12)
    r_s[...] = r
    v_s[...] = v
    al_s[...] = -kk
    ksum = jnp.zeros_like(k)
    for d in range(2):
        w_raw = vec_ref[d:d + 1, :] + _dot_f32(wl, w2_ref[d])
        z = -w_raw
        softplus = jnp.maximum(z, 0.0) + jnp.log(1.0 + jnp.exp(-jnp.abs(z)))
        lw_s[d] = -jnp.exp(-softplus - 0.5)
        iclr = _sigmoid(vec_ref[2 + d:3 + d, :] + _dot_f32(al, a2_ref[d]))
        kd = k * (1.0 + (iclr - 1.0) * vec_ref[5:6, :])
        kd_s[d] = kd
        be_s[d] = kk * iclr
        ksum = ksum + kd
    bonus = _dot_f32(r * ksum * vec_ref[6:7, :], ones)
    bv_ref[...] = bonus * v
    g_ref[...] = _dot(gl, g2_ref[...])

    def chunk(ci, carry):
        rows = pl.ds(pl.multiple_of(ci * CHUNK, CHUNK), CHUNK)
        for d in range(2):
            rm, y0, m, nn = _rwkv_unit(r_s[rows, :], v_s[rows, :], kd_s[d, rows, :], al_s[rows, :],
                                       be_s[d, rows, :], lw_s[d, rows, :], reverse=(d == 1))
            rm_ref[d, rows, :] = rm
            y0_ref[d, rows, :] = y0
            m_ref[d, rows, :] = m
            n_ref[d, rows, :] = nn
        return carry

    lax.fori_loop(0, RW_TILE // CHUNK, chunk, 0)


def _rwkv_precompute(rkv, lo, prm, *, n_ctx_rows, lc, ls):
    _, t, _ = rkv.shape
    n_tiles = t // RW_TILE
    halo = RW_TILE // 8
    last_halo = t // 8 - 1
    tile = lambda i, j: (j, i, 0)
    prev = lambda i, j: (j, jnp.maximum(i * halo - 1, 0), 0)
    nxt = lambda i, j: (j, jnp.minimum((i + 1) * halo, last_halo), 0)
    out_dir = jax.ShapeDtypeStruct((2, N_PAIRS, t, LANES), F32)
    out_one = jax.ShapeDtypeStruct((N_PAIRS, t, LANES), F32)
    dir_spec = pl.BlockSpec((2, None, RW_TILE, LANES), lambda i, j: (0, j, i, 0))
    one_spec = pl.BlockSpec((None, RW_TILE, LANES), tile)
    body = functools.partial(_rwkv_pre_body, lat_tiles=ls // RW_TILE, ctx_tiles=lc // RW_TILE,
                             n_ctx_tiles=n_ctx_rows // RW_TILE)
    return pl.pallas_call(
        body,
        grid=(n_tiles, N_PAIRS),
        in_specs=[pl.BlockSpec((None, RW_TILE, 3 * LANES), tile),
                  pl.BlockSpec((None, 8, 3 * LANES), prev),
                  pl.BlockSpec((None, 8, 3 * LANES), nxt),
                  pl.BlockSpec((RW_TILE, 3 * LANES), lambda i, j: (i, 0)),
                  pl.BlockSpec((8, 3 * LANES), lambda i, j: (jnp.maximum(i * halo - 1, 0), 0)),
                  pl.BlockSpec((8, 3 * LANES), lambda i, j: (jnp.minimum((i + 1) * halo, last_halo), 0)),
                  pl.BlockSpec((None, 8, 3 * LANES), lambda i, j: (j, 0, 0)),
                  pl.BlockSpec((8, 3 * LANES), lambda i, j: (0, 0)),
                  pl.BlockSpec((None, 16, LANES), lambda i, j: (j, 0, 0)),
                  pl.BlockSpec((2, None, LANES, LANES), lambda i, j: (0, j, 0, 0)),
                  pl.BlockSpec((2, None, LANES, LANES), lambda i, j: (0, j, 0, 0)),
                  pl.BlockSpec((None, LANES, LANES), lambda i, j: (j, 0, 0))],
        out_specs=[dir_spec, dir_spec, dir_spec, dir_spec, one_spec, one_spec],
        out_shape=[out_dir, out_dir, out_dir, out_dir, out_one, out_one],
        scratch_shapes=[pltpu.VMEM((RW_TILE, LANES), F32), pltpu.VMEM((RW_TILE, LANES), F32),
                        pltpu.VMEM((2, RW_TILE, LANES), F32), pltpu.VMEM((RW_TILE, LANES), F32),
                        pltpu.VMEM((2, RW_TILE, LANES), F32), pltpu.VMEM((2, RW_TILE, LANES), F32)],
        compiler_params=_cparams(("parallel", "parallel")),
        name="rwkv7_chunk_precompute",
    )(rkv, rkv, rkv, lo, lo, lo, prm["mu_rkv"], prm["mu_lo"], prm["vecs"], prm["w2"], prm["a2"], prm["g2"])


def _rwkv_scan_body(rm_c, y0_c, m_c, n_c, rm_l, y0_l, m_l, n_l, bv_c, g_c, bv_l, g_l, vec_ref,
                    oc_ref, ol_ref, yc_s, yl_s):
    def run(rm, y0, mm, nn, y_s, states):
        n_chunks = y_s.shape[1] // CHUNK

        def step(t, st):
            out = []
            for d in range(2):
                c = t if d == 0 else n_chunks - 1 - t
                rows = pl.ds(pl.multiple_of(c * CHUNK, CHUNK), CHUNK)
                both = _dot(jnp.concatenate([_embed(rm[d, rows, :]), _embed(mm[d, rows, :])], axis=0), st[d])
                y_s[d, rows, :] = _fold(both[:LANES]) + y0[d, rows, :]
                out.append(both[LANES:] + _embed(nn[d, rows, :]))
            return tuple(out)

        return lax.fori_loop(0, n_chunks, step, states)

    zero = jnp.zeros((LANES, LANES), F32)
    states = run(rm_c, y0_c, m_c, n_c, yc_s, (zero, zero))
    run(rm_l, y0_l, m_l, n_l, yl_s, states)

    ones = _head_ones()
    gn_w = vec_ref[7:8, :]
    gn_b = vec_ref[8:9, :]

    def finish(y_s, bv, g, o_ref):
        def tile(i, carry):
            rows = pl.ds(pl.multiple_of(i * RW_TILE, RW_TILE), RW_TILE)
            y = y_s[0, rows, :] + y_s[1, rows, :]
            dev = y - _dot_f32(y, ones) * (1.0 / HEAD_DIM)
            var = _dot_f32(dev * dev, ones) * (1.0 / HEAD_DIM)
            yn = dev * lax.rsqrt(var + GN_EPS) * gn_w + gn_b
            o_ref[rows, :] = ((yn + bv[rows, :]) * g[rows, :]).astype(o_ref.dtype)
            return carry

        lax.fori_loop(0, y_s.shape[1] // RW_TILE, tile, 0)

    finish(yc_s, bv_c, g_c, oc_ref)
    finish(yl_s, bv_l, g_l, ol_ref)


def _rwkv_scan(rm, y0, mm, nn, bv, g, vecs, *, nb, lc, ls):
    t = rm.shape[2]
    lat0 = nb * lc // ls
    ctx_d = pl.BlockSpec((2, None, lc, LANES), lambda b, j: (0, j, b, 0))
    lat_d = pl.BlockSpec((2, None, ls, LANES), lambda b, j: (0, j, lat0 + b, 0))
    ctx_1 = pl.BlockSpec((None, lc, LANES), lambda b, j: (j, b, 0))
    lat_1 = pl.BlockSpec((None, ls, LANES), lambda b, j: (j, lat0 + b, 0))
    out_c, out_l = pl.pallas_call(
        _rwkv_scan_body,
        grid=(nb, N_PAIRS),
        in_specs=[ctx_d, ctx_d, ctx_d, ctx_d, lat_d, lat_d, lat_d, lat_d, ctx_1, ctx_1, lat_1, lat_1,
                  pl.BlockSpec((None, 16, LANES), lambda b, j: (j, 0, 0))],
        out_specs=[pl.BlockSpec((lc, LANES), lambda b, j: (b, j)),
                   pl.BlockSpec((ls, LANES), lambda b, j: (b, j))],
        out_shape=[jax.ShapeDtypeStruct((nb * lc, GROUP_W), BF16),
                   jax.ShapeDtypeStruct((nb * ls, GROUP_W), BF16)],
        scratch_shapes=[pltpu.VMEM((2, lc, LANES), F32), pltpu.VMEM((2, ls, LANES), F32)],
        compiler_params=_cparams(("parallel", "parallel")),
        name="rwkv7_recurrence_output",
    )(rm, y0, mm, nn, rm, y0, mm, nn, bv, g, bv, g, vecs)
    return jnp.concatenate([out_c, out_l], axis=0)


def _gqa_group_attention(q_pairs, g, k, v, bias, sinks):
    rows = q_pairs[0].shape[0]
    qs = jnp.concatenate([_to_half(q_pairs[p], e, g) for p in range(2) for e in range(2)], axis=0)
    s = _dot_nt(qs, k)
    if bias is not None:
        s = s + jnp.tile(bias, (4, 1))
    sink = None
    if sinks is not None:
        sink = jnp.concatenate([jnp.broadcast_to(sinks[i], (rows, 1)) for i in range(4)], axis=0)
    o = _softmax_pv(s, v, sink)
    return [_merge_heads(o[(2 * p) * rows:(2 * p + 1) * rows], g, o[(2 * p + 1) * rows:(2 * p + 2) * rows], g)
            for p in range(2)]


def _window_body(q_ref, kp_ref, kc_ref, kn_ref, vp_ref, vc_ref, vn_ref, kx_ref, vx_ref,
                 cq_ref, sq_ref, cp_ref, sp_ref, cn_ref, sn_ref, sink_ref, o_ref, *, n_blocks, lc):
    n = pl.program_id(1)
    q = q_ref[...]
    k_lat = jnp.concatenate([_rope(kp_ref[...], cp_ref[...], sp_ref[...]),
                             _rope(kc_ref[...], cq_ref[...], sq_ref[...]),
                             _rope(kn_ref[...], cn_ref[...], sn_ref[...])], axis=0)
    k = jnp.concatenate([kx_ref[...], k_lat], axis=0).astype(BF16)
    v = jnp.concatenate([vx_ref[...], vp_ref[...], vc_ref[...], vn_ref[...]], axis=0).astype(BF16)
    nk = lc + 3 * BLOCK
    qi = lax.broadcasted_iota(jnp.int32, (BLOCK, nk), 0)
    kj = lax.broadcasted_iota(jnp.int32, (BLOCK, nk), 1) - lc
    kblk = n - 1 + kj // BLOCK
    ok = (kj < 0) | ((jnp.abs(kj - BLOCK - qi) <= WINDOW) & (kblk >= 0) & (kblk < n_blocks))
    bias = jnp.where(ok, 0.0, NEG_INF)
    for g in range(2):
        qp = [_rope(q[:, (2 * g + p) * LANES:(2 * g + p + 1) * LANES], cq_ref[...], sq_ref[...]) * ATTN_SCALE
              for p in range(2)]
        sinks = [sink_ref[4 * g + i:4 * g + i + 1, 0:1] for i in range(4)]
        outs = _gqa_group_attention(qp, g, k, v, bias, sinks)
        for p in range(2):
            o_ref[:, (2 * g + p) * LANES:(2 * g + p + 1) * LANES] = outs[p].astype(o_ref.dtype)


def _window_attention(qkv, cos, sin, sink, *, nb, lc, ls):
    n_blocks = ls // BLOCK
    base = nb * lc // BLOCK
    kcol, vcol = GROUP_W // LANES, GROUP_W // LANES + 1
    blk = lambda b, n, off: base + b * n_blocks + jnp.clip(n + off, 0, n_blocks - 1)
    kv_spec = lambda col, off: pl.BlockSpec((BLOCK, LANES), lambda b, n: (blk(b, n, off), col))
    rope_spec = lambda off: pl.BlockSpec((BLOCK, LANES), lambda b, n: (jnp.clip(n + off, 0, n_blocks - 1), 0))
    return pl.pallas_call(
        functools.partial(_window_body, n_blocks=n_blocks, lc=lc),
        grid=(nb, n_blocks),
        in_specs=[pl.BlockSpec((BLOCK, GROUP_W), lambda b, n: (blk(b, n, 0), 0)),
                  kv_spec(kcol, -1), kv_spec(kcol, 0), kv_spec(kcol, 1),
                  kv_spec(vcol, -1), kv_spec(vcol, 0), kv_spec(vcol, 1),
                  pl.BlockSpec((lc, LANES), lambda b, n: (b, kcol)),
                  pl.BlockSpec((lc, LANES), lambda b, n: (b, vcol)),
                  rope_spec(0), rope_spec(0), rope_spec(-1), rope_spec(-1), rope_spec(1), rope_spec(1),
                  pl.BlockSpec((8, LANES), lambda b, n: (0, 0))],
        out_specs=pl.BlockSpec((BLOCK, GROUP_W), lambda b, n: (b * n_blocks + n, 0)),
        out_shape=jax.ShapeDtypeStruct((nb * ls, GROUP_W), BF16),
        compiler_params=_cparams(("parallel", "parallel")),
        name="window_attention",
    )(qkv, qkv, qkv, qkv, qkv, qkv, qkv, qkv, qkv, cos, sin, cos, sin, cos, sin, sink)


def _global_body(q_ref, kx_ref, kl_ref, vx_ref, vl_ref, cq_ref, sq_ref, cos_ref, sin_ref, gq_ref, gk_ref,
                 o_ref, k_scr, v_scr, *, lc):
    @pl.when(pl.program_id(1) == 0)
    def _():
        k_scr[0:lc, :] = _head_rms(kx_ref[...], gk_ref[...]).astype(BF16)
        k_scr[lc:, :] = _rope(_head_rms(kl_ref[...], gk_ref[...]), cos_ref[...], sin_ref[...]).astype(BF16)
        v_scr[0:lc, :] = vx_ref[...].astype(BF16)
        v_scr[lc:, :] = vl_ref[...].astype(BF16)

    q = q_ref[...]
    k = k_scr[...]
    v = v_scr[...]
    for g in range(2):
        qp = [_rope(_head_rms(q[:, (2 * g + p) * LANES:(2 * g + p + 1) * LANES], gq_ref[...]),
                    cq_ref[...], sq_ref[...]) * ATTN_SCALE for p in range(2)]
        outs = _gqa_group_attention(qp, g, k, v, None, None)
        for p in range(2):
            o_ref[:, (2 * g + p) * LANES:(2 * g + p + 1) * LANES] = outs[p].astype(o_ref.dtype)


def _global_attention(qkv, cos, sin, q_gain, k_gain, *, nb, lc, ls):
    n_blocks = ls // BLOCK
    base = nb * lc // BLOCK
    lat0 = nb * lc // ls
    kcol, vcol = GROUP_W // LANES, GROUP_W // LANES + 1
    return pl.pallas_call(
        functools.partial(_global_body, lc=lc),
        grid=(nb, n_blocks),
        in_specs=[pl.BlockSpec((BLOCK, GROUP_W), lambda b, n: (base + b * n_blocks + n, 0)),
                  pl.BlockSpec((lc, LANES), lambda b, n: (b, kcol)),
                  pl.BlockSpec((ls, LANES), lambda b, n: (lat0 + b, kcol)),
                  pl.BlockSpec((lc, LANES), lambda b, n: (b, vcol)),
                  pl.BlockSpec((ls, LANES), lambda b, n: (lat0 + b, vcol)),
                  pl.BlockSpec((BLOCK, LANES), lambda b, n: (n, 0)),
                  pl.BlockSpec((BLOCK, LANES), lambda b, n: (n, 0)),
                  pl.BlockSpec((ls, LANES), lambda b, n: (0, 0)),
                  pl.BlockSpec((ls, LANES), lambda b, n: (0, 0)),
                  pl.BlockSpec((1, LANES), lambda b, n: (0, 0)),
                  pl.BlockSpec((1, LANES), lambda b, n: (0, 0))],
        out_specs=pl.BlockSpec((BLOCK, GROUP_W), lambda b, n: (b * n_blocks + n, 0)),
        out_shape=jax.ShapeDtypeStruct((nb * ls, GROUP_W), BF16),
        scratch_shapes=[pltpu.VMEM((lc + ls, LANES), BF16), pltpu.VMEM((lc + ls, LANES), BF16)],
        compiler_params=_cparams(("parallel", "arbitrary")),
        name="global_attention",
    )(qkv, qkv, qkv, qkv, qkv, cos, sin, cos, sin, q_gain, k_gain)


def _na_body(q_ref, kx_ref, kl_ref, vx_ref, vl_ref, bias_ref, o_ref, *, n_rows, lc):
    r = pl.program_id(1)
    kh = min(NA_KH, n_rows)
    rs = jnp.clip(r - kh // 2, 0, n_rows - kh)
    win = pl.ds(pl.multiple_of(rs * GRID_W, GRID_W), kh * GRID_W)
    q = q_ref[...] * ATTN_SCALE
    for p in range(N_PAIRS):
        cols = slice(p * LANES, (p + 1) * LANES)
        k = jnp.concatenate([kx_ref[:, cols], kl_ref[win, cols]], axis=0).astype(BF16)
        v = jnp.concatenate([vx_ref[:, cols], vl_ref[win, cols]], axis=0).astype(BF16)
        s = _dot_nt(_embed(q[:, cols]), k)
        s = s + jnp.concatenate([jnp.zeros((2 * GRID_W, lc), F32), bias_ref[p]], axis=1)
        o = _softmax_pv(s, v)
        o_ref[:, cols] = jnp.where(_lane_half((GRID_W, LANES)), o[:GRID_W], o[GRID_W:]).astype(o_ref.dtype)


def _na_bias_table(rpb, n_rows):
    kh = min(NA_KH, n_rows)
    qc = np.arange(GRID_W)[:, None]
    kc = np.arange(GRID_W)[None, :]
    win_start = np.clip(qc - NA_KW // 2, 0, GRID_W - NA_KW)
    col_ok = (kc >= win_start) & (kc < win_start + NA_KW)
    dc = np.clip(kc - qc + NA_KW - 1, 0, 2 * NA_KW - 2)
    var = np.arange(kh)[:, None]
    dr = np.arange(kh)[None, :] - var + NA_KH - 1
    tab = rpb[:, dr[:, :, None, None], dc[None, None, :, :]]
    tab = jnp.where(col_ok[None, None, None], tab, NEG_INF)
    tab = tab.transpose(1, 0, 3, 2, 4).reshape(kh, N_PAIRS, 2 * GRID_W, kh * GRID_W)
    return tab.astype(F32)


def _neighbourhood_attention(qkv, bias, *, nb, lc, ls):
    n_rows = ls // GRID_W
    kh = min(NA_KH, n_rows)
    base = nb * lc // GRID_W
    lat0 = nb * lc // ls

    def variant(b, r):
        return (r - jnp.clip(r - kh // 2, 0, n_rows - kh), 0, 0, 0)

    return pl.pallas_call(
        functools.partial(_na_body, n_rows=n_rows, lc=lc),
        grid=(nb, n_rows),
        in_specs=[pl.BlockSpec((GRID_W, GROUP_W), lambda b, r: (base + b * n_rows + r, 0)),
                  pl.BlockSpec((lc, GROUP_W), lambda b, r: (b, 1)),
                  pl.BlockSpec((ls, GROUP_W), lambda b, r: (lat0 + b, 1)),
                  pl.BlockSpec((lc, GROUP_W), lambda b, r: (b, 2)),
                  pl.BlockSpec((ls, GROUP_W), lambda b, r: (lat0 + b, 2)),
                  pl.BlockSpec((None, N_PAIRS, 2 * GRID_W, kh * GRID_W), variant)],
        out_specs=pl.BlockSpec((GRID_W, GROUP_W), lambda b, r: (b * n_rows + r, 0)),
        out_shape=jax.ShapeDtypeStruct((nb * ls, GROUP_W), BF16),
        compiler_params=_cparams(("parallel", "parallel")),
        name="neighbourhood_attention",
    )(qkv, qkv, qkv, qkv, qkv, bias)


def _ctx_body(q_ref, k_ref, v_ref, sink_ref, gq_ref, gk_ref, o_ref, *, gqa, use_sink, use_norm):
    q = q_ref[...]
    k = k_ref[...]
    v = v_ref[...].astype(BF16)
    if gqa:
        if use_norm:
            k = _head_rms(k, gk_ref[...])
        k = k.astype(BF16)
        for g in range(2):
            qp = []
            for p in range(2):
                x = q[:, (2 * g + p) * LANES:(2 * g + p + 1) * LANES]
                if use_norm:
                    x = _head_rms(x, gq_ref[...])
                qp.append(x * ATTN_SCALE)
            sinks = [sink_ref[4 * g + i:4 * g + i + 1, 0:1] for i in range(4)] if use_sink else None
            outs = _gqa_group_attention(qp, g, k, v, None, sinks)
            for p in range(2):
                o_ref[:, (2 * g + p) * LANES:(2 * g + p + 1) * LANES] = outs[p].astype(o_ref.dtype)
    else:
        rows = q.shape[0]
        for p in range(N_PAIRS):
            cols = slice(p * LANES, (p + 1) * LANES)
            o = _softmax_pv(_dot_nt(_embed(q[:, cols] * ATTN_SCALE), k[:, cols]), v[:, cols])
            o_ref[:, cols] = jnp.where(_lane_half((rows, LANES)), o[:rows], o[rows:]).astype(o_ref.dtype)


def _ctx_attention(qkv, *, nb, lc, gqa, q_col=0, sink=None, q_gain=None, k_gain=None):
    kw = KV_W if gqa else GROUP_W
    kcol = (q_col * GROUP_W + GROUP_W) // kw
    dummy = jnp.zeros((8, LANES), F32)
    one = jnp.ones((1, LANES), F32)
    return pl.pallas_call(
        functools.partial(_ctx_body, gqa=gqa, use_sink=sink is not None, use_norm=q_gain is not None),
        grid=(nb,),
        in_specs=[pl.BlockSpec((lc, GROUP_W), lambda b: (b, q_col)),
                  pl.BlockSpec((lc, kw), lambda b: (b, kcol)),
                  pl.BlockSpec((lc, kw), lambda b: (b, kcol + 1)),
                  pl.BlockSpec((8, LANES), lambda b: (0, 0)),
                  pl.BlockSpec((1, LANES), lambda b: (0, 0)),
                  pl.BlockSpec((1, LANES), lambda b: (0, 0))],
        out_specs=pl.BlockSpec((lc, GROUP_W), lambda b: (b, 0)),
        out_shape=jax.ShapeDtypeStruct((nb * lc, GROUP_W), BF16),
        compiler_params=_cparams(("parallel",)),
        name="context_attention",
    )(qkv, qkv, qkv, dummy if sink is None else sink, one if q_gain is None else q_gain,
      one if k_gain is None else k_gain)


def _rope_tables(ls):
    t = jnp.arange(ls, dtype=jnp.int32)
    n_freq = HEAD_DIM // 4
    inv_freq = ROPE_BASE ** (-jnp.arange(n_freq, dtype=F32) / n_freq)
    row = (t // GRID_W).astype(F32)[:, None] * inv_freq
    col = (t % GRID_W).astype(F32)[:, None] * inv_freq
    cos = jnp.concatenate([jnp.cos(row), jnp.cos(row), jnp.cos(col), jnp.cos(col)], axis=-1)
    sin = jnp.concatenate([-jnp.sin(row), jnp.sin(row), -jnp.sin(col), jnp.sin(col)], axis=-1)
    return jnp.tile(cos, (1, 2)), jnp.tile(sin, (1, 2))


def _rwkv_params(mu_prev, mu_next, w0, w2, a0, a2, g2, k_k, k_a, r_k, gn_w, gn_b):
    aw = GROUP_W
    pair = lambda x: x.reshape(N_PAIRS, LANES)

    def mu_rkv(mu):
        return jnp.concatenate([pair(mu[0:aw]), pair(mu[aw:2 * aw]), pair(mu[2 * aw:3 * aw])], axis=-1)

    mu_a = jnp.zeros((N_PAIRS, 8, 3 * LANES), F32)
    mu_a = mu_a.at[:, 0].set(mu_rkv(mu_prev)).at[:, 1].set(mu_rkv(mu_next))
    mu_l = jnp.zeros((8, 3 * LANES), F32).at[0].set(mu_prev[3 * aw:]).at[1].set(mu_next[3 * aw:])
    vecs = jnp.zeros((N_PAIRS, 16, LANES), F32)
    rows = [pair(w0[0]), pair(w0[1]), pair(a0[0]), pair(a0[1]), pair(k_k), pair(k_a), pair(r_k.reshape(-1)),
            pair(gn_w), pair(gn_b)]
    for i, val in enumerate(rows):
        vecs = vecs.at[:, i].set(val)

    def lora_pad(w):
        out = jnp.zeros((2, N_PAIRS, LANES, LANES), F32)
        for d in range(2):
            blk = w[d].reshape(DECAY_LORA, N_PAIRS, LANES).transpose(1, 0, 2)
            out = out.at[d, :, d * DECAY_LORA:(d + 1) * DECAY_LORA, :].set(blk)
        return out

    g2p = g2.reshape(GATE_LORA, N_PAIRS, LANES).transpose(1, 0, 2)
    return dict(mu_rkv=mu_a, mu_lo=mu_l, vecs=vecs, w2=lora_pad(w2), a2=lora_pad(a2), g2=g2p)


def _lane_bcast(x, rows):
    return jnp.zeros((rows, LANES), F32).at[:x.shape[0]].set(jnp.broadcast_to(x[:, None], (x.shape[0], LANES)))


A_W = GROUP_W
A_IN = 3 * A_W + 3 * LANES
B_IN = GROUP_W + 2 * KV_W


def _even_mixer(h, mods_l, gain, w_in, a_prm, sink, cos, sin, *, dims):
    nb, lc, ls = dims["nb"], dims["lc"], dims["ls"]
    t = h.shape[0]
    rkv_dests = [(0, j, part * LANES, part * A_W + j * LANES, LANES)
                 for j in range(N_PAIRS) for part in range(3)]
    pieces = [(0, 3 * A_W, rkv_dests),
              (3 * A_W, 3 * LANES, [(1, None, 0, 0, 3 * LANES)]),
              (A_IN, B_IN, [(2, None, 0, 0, B_IN)])]
    rkv, lo, qkv = _in_proj(h, mods_l, gain, w_in, [(N_PAIRS, t, 3 * LANES), (t, 3 * LANES), (t, B_IN)], pieces,
                            tm=dims["tm"], n_ctx_rows=nb * lc, ls=ls, nb=nb)
    rm, y0, mm, nn, bv, g = _rwkv_precompute(rkv, lo, a_prm, n_ctx_rows=nb * lc, lc=lc, ls=ls)
    mix_a = _rwkv_scan(rm, y0, mm, nn, bv, g, a_prm["vecs"], nb=nb, lc=lc, ls=ls)
    sink_b = _lane_bcast(sink, 8)
    b_lat = _window_attention(qkv, cos, sin, sink_b, nb=nb, lc=lc, ls=ls)
    b_ctx = _ctx_attention(qkv, nb=nb, lc=lc, gqa=True, sink=sink_b)
    return mix_a, jnp.concatenate([b_ctx, b_lat], axis=0)


def _odd_mixer(h, mods_l, gain, w_in, rpb, q_gain, k_gain, cos, sin, *, dims):
    nb, lc, ls = dims["nb"], dims["lc"], dims["ls"]
    t = h.shape[0]
    c_in = 3 * GROUP_W
    pieces = [(0, c_in, [(0, None, 0, 0, c_in)]), (c_in, B_IN, [(1, None, 0, 0, B_IN)])]
    qkv_c, qkv_d = _in_proj(h, mods_l, gain, w_in, [(t, c_in), (t, B_IN)], pieces,
                            tm=dims["tm"], n_ctx_rows=nb * lc, ls=ls, nb=nb)
    bias = _na_bias_table(rpb, ls // GRID_W)
    c_lat = _neighbourhood_attention(qkv_c, bias, nb=nb, lc=lc, ls=ls)
    c_ctx = _ctx_attention(qkv_c, nb=nb, lc=lc, gqa=False)
    gq = jnp.tile(q_gain, 2)[None, :]
    gk = jnp.tile(k_gain, 2)[None, :]
    d_lat = _global_attention(qkv_d, cos, sin, gq, gk, nb=nb, lc=lc, ls=ls)
    d_ctx = _ctx_attention(qkv_d, nb=nb, lc=lc, gqa=True, q_gain=gq, k_gain=gk)
    return jnp.concatenate([c_ctx, c_lat], axis=0), jnp.concatenate([d_ctx, d_lat], axis=0)


def kernel(x, c, ctx, c_ctx, w_ada, b_ada, g_pre_mix, g_post_mix, g_pre_ff, g_post_ff, w_in_even, w_in_odd, w_out, w_ff1, w_ff2, a_mu_prev, a_mu_next, a_w0, a_w2, a_a0, a_a2, a_g2, a_k_k, a_k_a, a_r_k, a_gn_w, a_gn_b, b_sink, c_rpb, d_q_gain, d_k_gain):
    nb, ls, d = x.shape
    lc = ctx.shape[1]
    depth = w_ada.shape[0]
    assert nb < 16 and lc % RW_TILE == 0 and ls % RW_TILE == 0 and (nb * lc) % ls == 0
    dims = dict(nb=nb, lc=lc, ls=ls, tm=256)
    n_ctx_rows = nb * lc
    cvec = jnp.zeros((16, d), F32).at[:nb].set(c).at[nb].set(c_ctx)
    mods = _modulation(cvec, w_ada, b_ada)
    cos, sin = _rope_tables(ls)
    h = jnp.concatenate([ctx.reshape(n_ctx_rows, d), x.reshape(nb * ls, d)], axis=0)
    for i in range(depth):
        j = i // 2
        last = i == depth - 1
        if i % 2 == 0:
            a_prm = _rwkv_params(a_mu_prev[j], a_mu_next[j], a_w0[j], a_w2[j], a_a0[j], a_a2[j], a_g2[j],
                                 a_k_k[j], a_k_a[j], a_r_k[j], a_gn_w[j], a_gn_b[j])
            mix_a, mix_b = _even_mixer(h, mods[i], g_pre_mix[i], w_in_even[j].astype(BF16), a_prm, b_sink[j],
                                       cos, sin, dims=dims)
        else:
            mix_a, mix_b = _odd_mixer(h, mods[i], g_pre_mix[i], w_in_odd[j].astype(BF16), c_rpb[j],
                                      d_q_gain[j], d_k_gain[j], cos, sin, dims=dims)
        h = _out_proj(mix_a, mix_b, h, mods[i], g_post_mix[i], w_out[i].astype(BF16),
                      tm=dims["tm"], n_ctx_rows=n_ctx_rows, ls=ls, nb=nb)
        h = _mlp(h, mods[i], g_pre_ff[i], g_post_ff[i], w_ff1[i].astype(BF16), w_ff2[i].astype(BF16),
                 tm=dims["tm"], fc=1024, n_ctx_rows=n_ctx_rows, ls=ls, nb=nb,
                 first_row=n_ctx_rows if last else 0)
    return h.reshape(nb, ls, d)
```

```python
import functools

import numpy as np
import jax
import jax.numpy as jnp
from jax import lax
from jax.experimental import pallas as pl
from jax.experimental.pallas import tpu as pltpu

F32 = jnp.float32
BF16 = jnp.bfloat16

HEAD_DIM = 64
LANES = 128
N_PAIRS = 4
GROUP_W = N_PAIRS * LANES
KV_W = LANES
DECAY_LORA = 64
GATE_LORA = 128
WINDOW = 128
BLOCK = 128
GRID_W = 64
NA_KH = 8
NA_KW = 16
ROPE_BASE = 10000.0
NORM_EPS = 1e-6
GN_EPS = 64e-5
NEG_INF = -1e30
ATTN_SCALE = HEAD_DIM ** -0.5
CHUNK = 64
RW_TILE = 256
VMEM_LIMIT = 56 * 1024 * 1024


def _cparams(sem):
    return pltpu.CompilerParams(dimension_semantics=sem, vmem_limit_bytes=VMEM_LIMIT)


def _dot(a, b):
    return jnp.dot(a.astype(BF16), b.astype(BF16), preferred_element_type=F32)


def _dot_nt(a, b):
    return lax.dot_general(a.astype(BF16), b.astype(BF16), (((1,), (1,)), ((), ())),
                           preferred_element_type=F32)


def _dot_tn(a, b):
    return lax.dot_general(a.astype(BF16), b.astype(BF16), (((0,), (0,)), ((), ())),
                           preferred_element_type=F32)


def _dot_f32(a, b):
    return jnp.dot(a, b, preferred_element_type=F32, precision=lax.Precision.HIGHEST)


def _dot_exact_rhs(x, b, pieces):
    b = b.astype(BF16)
    acc = None
    for _ in range(pieces):
        part = x.astype(BF16)
        term = jnp.dot(part, b, preferred_element_type=F32)
        acc = term if acc is None else acc + term
        x = x - part.astype(F32)
    return acc


def _dot_exact_lhs(a, x, pieces):
    a = a.astype(BF16)
    acc = None
    for _ in range(pieces):
        part = x.astype(BF16)
        term = jnp.dot(a, part, preferred_element_type=F32)
        acc = term if acc is None else acc + term
        x = x - part.astype(F32)
    return acc


def _lane_half(shape):
    return lax.broadcasted_iota(jnp.int32, shape, len(shape) - 1) % LANES < HEAD_DIM


def _head_ones():
    r = lax.broadcasted_iota(jnp.int32, (LANES, LANES), 0) // HEAD_DIM
    c = lax.broadcasted_iota(jnp.int32, (LANES, LANES), 1) // HEAD_DIM
    return (r == c).astype(F32)


def _embed(x):
    first = _lane_half(x.shape)
    zero = jnp.zeros_like(x)
    return jnp.concatenate([jnp.where(first, x, zero), jnp.where(first, zero, x)], axis=0)


def _fold(x):
    n = x.shape[0] // 2
    return x[:n] + x[n:]


def _sigmoid(x):
    return 1.0 / (1.0 + jnp.exp(-x))


def _rms_rows(x, gain):
    return x * lax.rsqrt(jnp.mean(x * x, axis=-1, keepdims=True) + NORM_EPS) * gain


def _rope(x, cos, sin):
    lane = lax.broadcasted_iota(jnp.int32, x.shape, 1)
    swapped = jnp.where(lane % 32 < 16, pltpu.roll(x, LANES - 16, 1), pltpu.roll(x, 16, 1))
    return x * cos + swapped * sin


def _head_rms(x, gain):
    ms = _dot_exact_rhs(x * x, _head_ones(), 2) * (1.0 / HEAD_DIM)
    return x * lax.rsqrt(ms + NORM_EPS) * gain


def _to_half(x, src, dst):
    if src != dst:
        x = pltpu.roll(x, HEAD_DIM, 1)
    first = _lane_half(x.shape)
    keep = first if dst == 0 else jnp.logical_not(first)
    return jnp.where(keep, x, jnp.zeros_like(x))


def _softmax_pv(s, v, sink=None):
    m = jnp.max(s, axis=-1, keepdims=True)
    if sink is not None:
        m = jnp.maximum(m, sink)
    p = jnp.exp(s - m)
    den = jnp.sum(p, axis=-1, keepdims=True)
    if sink is not None:
        den = den + jnp.exp(sink - m)
    return _dot(p, v) / den


def _merge_heads(o0, src0, o1, src1):
    a = o0 if src0 == 0 else pltpu.roll(o0, HEAD_DIM, 1)
    b = o1 if src1 == 1 else pltpu.roll(o1, HEAD_DIM, 1)
    return jnp.where(_lane_half(a.shape), a, b)


def _mod_body(c_ref, w_ref, b_ref, o_ref):
    c = c_ref[...]
    s = c * _sigmoid(c)
    o_ref[...] = _dot_f32(s, w_ref[...]) + b_ref[...]


def _modulation(cvec, w_ada, b_ada):
    depth, d, _ = w_ada.shape
    out = pl.pallas_call(
        _mod_body,
        grid=(depth, 6),
        in_specs=[pl.BlockSpec((16, d), lambda l, n: (0, 0)),
                  pl.BlockSpec((None, d, d), lambda l, n: (l, 0, n)),
                  pl.BlockSpec((None, None, 1, d), lambda l, n: (l, n, 0, 0))],
        out_specs=pl.BlockSpec((None, None, 16, d), lambda l, n: (l, n, 0, 0)),
        out_shape=jax.ShapeDtypeStruct((depth, 6, 16, d), F32),
        compiler_params=_cparams(("parallel", "parallel")),
        name="adaln_modulation",
    )(cvec, w_ada, b_ada.reshape(depth, 6, 1, d))
    return out.transpose(0, 2, 1, 3)


def _mod_row(tile, tm, n_ctx_rows, ls, nb):
    start = tile * tm
    return jnp.where(start < n_ctx_rows, nb, (start - n_ctx_rows) // ls)


def _inproj_body(x_ref, m_ref, g_ref, w_ref, *o_refs, pieces):
    u = _rms_rows(x_ref[...], g_ref[...]) * (1.0 + m_ref[1:2, :]) + m_ref[0:1, :]
    u = u.astype(BF16)
    for (src, span, dests) in pieces:
        val = jnp.dot(u, w_ref[:, src:src + span], preferred_element_type=F32)
        for (oi, lead, dst, off, width) in dests:
            if lead is None:
                o_refs[oi][:, dst:dst + width] = val[:, off:off + width]
            else:
                o_refs[oi][lead, :, dst:dst + width] = val[:, off:off + width]


def _in_proj(h, mods_l, gain, w, out_defs, pieces, *, tm, n_ctx_rows, ls, nb):
    t, d = h.shape
    n_in = w.shape[1]
    out_shapes, out_specs = [], []
    for shape in out_defs:
        out_shapes.append(jax.ShapeDtypeStruct(shape, F32))
        if len(shape) == 3:
            out_specs.append(pl.BlockSpec((shape[0], tm, shape[2]), lambda i: (0, i, 0)))
        else:
            out_specs.append(pl.BlockSpec((tm, shape[1]), lambda i: (i, 0)))
    row = functools.partial(_mod_row, tm=tm, n_ctx_rows=n_ctx_rows, ls=ls, nb=nb)
    return pl.pallas_call(
        functools.partial(_inproj_body, pieces=pieces),
        grid=(t // tm,),
        in_specs=[pl.BlockSpec((tm, d), lambda i: (i, 0)),
                  pl.BlockSpec((None, 6, d), lambda i: (row(i), 0, 0)),
                  pl.BlockSpec((1, d), lambda i: (0, 0)),
                  pl.BlockSpec((d, n_in), lambda i: (0, 0))],
        out_specs=out_specs,
        out_shape=out_shapes,
        compiler_params=_cparams(("parallel",)),
        name="modulate_in_proj",
    )(h, mods_l, gain.reshape(1, d), w)


def _outproj_body(a_ref, b_ref, h_ref, m_ref, g_ref, w_ref, o_ref):
    o = (jnp.dot(a_ref[...], w_ref[0:GROUP_W, :], preferred_element_type=F32)
         + jnp.dot(b_ref[...], w_ref[GROUP_W:2 * GROUP_W, :], preferred_element_type=F32))
    o_ref[...] = h_ref[...] + m_ref[2:3, :] * _rms_rows(o, g_ref[...])


def _out_proj(mix_a, mix_b, h, mods_l, gain, w, *, tm, n_ctx_rows, ls, nb):
    t, d = h.shape
    row = functools.partial(_mod_row, tm=tm, n_ctx_rows=n_ctx_rows, ls=ls, nb=nb)
    return pl.pallas_call(
        _outproj_body,
        grid=(t // tm,),
        in_specs=[pl.BlockSpec((tm, GROUP_W), lambda i: (i, 0)),
                  pl.BlockSpec((tm, GROUP_W), lambda i: (i, 0)),
                  pl.BlockSpec((tm, d), lambda i: (i, 0)),
                  pl.BlockSpec((None, 6, d), lambda i: (row(i), 0, 0)),
                  pl.BlockSpec((1, d), lambda i: (0, 0)),
                  pl.BlockSpec((2 * GROUP_W, d), lambda i: (0, 0))],
        out_specs=pl.BlockSpec((tm, d), lambda i: (i, 0)),
        out_shape=jax.ShapeDtypeStruct((t, d), F32),
        compiler_params=_cparams(("parallel",)),
        name="out_proj_gate_residual",
    )(mix_a, mix_b, h, mods_l, gain.reshape(1, d), w)


def _mlp_body(x_ref, m_ref, g1_ref, g2_ref, w1_ref, w2_ref, o_ref, u_scr, acc_scr):
    k = pl.program_id(1)

    @pl.when(k == 0)
    def _():
        u = _rms_rows(x_ref[...], g1_ref[...]) * (1.0 + m_ref[4:5, :]) + m_ref[3:4, :]
        u_scr[...] = u.astype(BF16)
        acc_scr[...] = jnp.zeros_like(acc_scr)

    a = jnp.dot(u_scr[...], w1_ref[...].astype(BF16), preferred_element_type=F32)
    a = jnp.square(jnp.maximum(a, 0.0))
    acc_scr[...] += jnp.dot(a.astype(BF16), w2_ref[...].astype(BF16), preferred_element_type=F32)

    @pl.when(k == pl.num_programs(1) - 1)
    def _():
        o_ref[...] = x_ref[...] + m_ref[5:6, :] * _rms_rows(acc_scr[...], g2_ref[...])


def _mlp(h, mods_l, g_pre, g_post, w1, w2, *, tm, fc, n_ctx_rows, ls, nb, first_row):
    t, d = h.shape
    dff = w1.shape[1]
    off = first_row // tm
    row = functools.partial(_mod_row, tm=tm, n_ctx_rows=n_ctx_rows, ls=ls, nb=nb)
    return pl.pallas_call(
        _mlp_body,
        grid=((t - first_row) // tm, dff // fc),
        in_specs=[pl.BlockSpec((tm, d), lambda i, k: (i + off, 0)),
                  pl.BlockSpec((None, 6, d), lambda i, k: (row(i + off), 0, 0)),
                  pl.BlockSpec((1, d), lambda i, k: (0, 0)),
                  pl.BlockSpec((1, d), lambda i, k: (0, 0)),
                  pl.BlockSpec((d, fc), lambda i, k: (0, k)),
                  pl.BlockSpec((fc, d), lambda i, k: (k, 0))],
        out_specs=pl.BlockSpec((tm, d), lambda i, k: (i, 0)),
        out_shape=jax.ShapeDtypeStruct((t - first_row, d), F32),
        scratch_shapes=[pltpu.VMEM((tm, d), BF16), pltpu.VMEM((tm, d), F32)],
        compiler_params=_cparams(("parallel", "arbitrary")),
        name="sqrelu_mlp_gate_residual",
    )(h, mods_l, g_pre.reshape(1, d), g_post.reshape(1, d), w1, w2)


def _rwkv_units(units):
    n = units[0][0].shape[0]
    n2 = 2 * n
    rev = [u[6] for u in units]
    ri = lax.broadcasted_iota(jnp.int32, (n, n), 0)
    ci = lax.broadcasted_iota(jnp.int32, (n, n), 1)
    tri = {False: (ci <= ri).astype(F32), True: (ci >= ri).astype(F32)}
    rr = lax.broadcasted_iota(jnp.int32, (n2, n2), 0)
    cc = lax.broadcasted_iota(jnp.int32, (n2, n2), 1)
    r2, c2 = rr % n, cc % n
    incl2 = {False: c2 <= r2, True: c2 >= r2}
    strict2 = {False: c2 < r2, True: c2 > r2}
    eye = rr == cc
    eye_f = jnp.where(eye, 1.0, 0.0)
    zero = jnp.zeros((n2, n2), F32)
    base = 8
    diag_blk = rr // base == cc // base
    sizes = []
    size = base
    while size < n:
        sizes.append(size)
        size *= 2
    off_blk = [(rr // (2 * s) == cc // (2 * s)) & (rr // s != cc // s) for s in sizes]

    cs = [_dot_exact_lhs(tri[u[6]], u[5], 3) for u in units]
    ops = []
    for (r, v, kd, alpha, beta, logw, _), c in zip(units, cs):
        c_all = jnp.sum(logw, axis=0, keepdims=True)
        e_pos = jnp.exp(c)
        e_neg = jnp.exp(-c)
        e_end = jnp.exp(c_all - c)
        ops.append(dict(rt=_embed(r * e_pos), at=_embed(alpha * jnp.exp(c - logw)), kt=_embed(kd * e_neg),
                        bt=_embed(beta * e_neg), kh=_embed(kd * e_end), bh=_embed(beta * e_end), vb=_embed(v),
                        decay=jnp.exp(c_all)))
    a_all = [_dot_nt(jnp.concatenate([o["rt"], o["at"]], axis=0), jnp.concatenate([o["kt"], o["bt"]], axis=0))
             for o in ops]
    a_qk = [jnp.where(incl2[f], a[:n2, :n2], zero) for a, f in zip(a_all, rev)]
    a_qb = [jnp.where(incl2[f], a[:n2, n2:], zero) for a, f in zip(a_all, rev)]
    a_ak = [jnp.where(strict2[f], a[n2:, :n2], zero) for a, f in zip(a_all, rev)]
    a_ab = [jnp.where(strict2[f], a[n2:, n2:], zero) for a, f in zip(a_all, rev)]
    akv = [_dot(a, o["vb"]) for a, o in zip(a_ak, ops)]
    apow = [jnp.where(diag_blk, a, zero) for a in a_ab]
    inv = [eye_f + a for a in apow]
    for _ in range(2):
        apow = [_dot(a, a) for a in apow]
        inv = [x + _dot(x, a) for x, a in zip(inv, apow)]
    for blk in off_blk:
        right = [_dot(jnp.where(blk, a, zero), x) for a, x in zip(a_ab, inv)]
        inv = [x + _dot(x, y) for x, y in zip(inv, right)]
    wu = [_dot(x, jnp.concatenate([o["at"], y], axis=1)) for x, o, y in zip(inv, ops, akv)]
    outs = []
    for o, x, qb, qk in zip(ops, wu, a_qb, a_qk):
        w = x[:, :LANES]
        u0v = jnp.concatenate([x[:, LANES:], o["vb"]], axis=0)
        m = jnp.where(eye, o["decay"], 0.0) + _dot_tn(o["bh"], w)
        nn = _dot_tn(jnp.concatenate([o["bh"], o["kh"]], axis=0), u0v)
        rm = o["rt"] + _dot(qb, w)
        y0 = _dot(jnp.concatenate([qb, qk], axis=1), u0v)
        outs.append((_fold(rm), _fold(y0), _fold(m), _fold(nn)))
    return outs


def _shifted(x, prev_row, next_row, mu_prev, mu_next):
    rows = x.shape[0]
    ridx = lax.broadcasted_iota(jnp.int32, x.shape, 0)
    prev = jnp.where(ridx == 0, prev_row, pltpu.roll(x, 1, 0))
    nxt = jnp.where(ridx == rows - 1, next_row, pltpu.roll(x, rows - 1, 0))
    return x + mu_prev * (prev - x) + mu_next * (nxt - x)


def _rwkv_pre_body(rkv_ref, rkv_p_ref, rkv_n_ref, lo_ref, lo_p_ref, lo_n_ref,
                   mu_rkv_ref, mu_lo_ref, vec_ref, w2_ref, a2_ref, g2_ref,
                   rm_ref, y0_ref, m_ref, n_ref, bv_ref, g_ref,
                   r_s, v_s, kd_s, al_s, be_s, lw_s, *, lat_tiles, ctx_tiles, n_ctx_tiles):
    i = pl.program_id(0)
    lat = i >= n_ctx_tiles
    pos = jnp.where(lat, (i - n_ctx_tiles) % lat_tiles, i % ctx_tiles)
    has_prev = (pos != 0).astype(F32)
    has_next = (pos != jnp.where(lat, lat_tiles, ctx_tiles) - 1).astype(F32)
    rkv = _shifted(rkv_ref[...], rkv_p_ref[7:8, :] * has_prev, rkv_n_ref[0:1, :] * has_next,
                   mu_rkv_ref[0:1, :], mu_rkv_ref[1:2, :])
    lo = _shifted(lo_ref[...], lo_p_ref[7:8, :] * has_prev, lo_n_ref[0:1, :] * has_next,
                  mu_lo_ref[0:1, :], mu_lo_ref[1:2, :])
    r = rkv[:, 0:LANES]
    k = rkv[:, LANES:2 * LANES]
    v = rkv[:, 2 * LANES:3 * LANES]
    wl = jnp.tanh(lo[:, 0:LANES])
    al = lo[:, LANES:2 * LANES]
    gl = _sigmoid(lo[:, 2 * LANES:3 * LANES])
    ones = _head_ones()
    kx = k * vec_ref[4:5, :]
    kk = kx / jnp.maximum(jnp.sqrt(_dot_exact_rhs(kx * kx, ones, 2)), 1e-12)
    r_s[...] = r
    v_s[...] = v
    al_s[...] = -kk
    ksum = jnp.zeros_like(k)
    for d in range(2):
        w_raw = vec_ref[d:d + 1, :] + _dot(wl, w2_ref[d])
        z = -w_raw
        softplus = jnp.maximum(z, 0.0) + jnp.log(1.0 + jnp.exp(-jnp.abs(z)))
        lw_s[d] = -jnp.exp(-softplus - 0.5)
        iclr = _sigmoid(vec_ref[2 + d:3 + d, :] + _dot(al, a2_ref[d]))
        kd = k * (1.0 + (iclr - 1.0) * vec_ref[5:6, :])
        kd_s[d] = kd
        be_s[d] = kk * iclr
        ksum = ksum + kd
    bonus = _dot_exact_rhs(r * ksum * vec_ref[6:7, :], ones, 2)
    bv_ref[...] = bonus * v
    g_ref[...] = _dot(gl, g2_ref[...])

    units = []
    for ci in range(RW_TILE // CHUNK):
        rows = slice(ci * CHUNK, (ci + 1) * CHUNK)
        for d in range(2):
            units.append((r_s[rows, :], v_s[rows, :], kd_s[d, rows, :], al_s[rows, :], be_s[d, rows, :],
                          lw_s[d, rows, :], d == 1))
    outs = _rwkv_units(units)
    for idx, (rm, y0, m, nn) in enumerate(outs):
        ci, d = divmod(idx, 2)
        rows = slice(ci * CHUNK, (ci + 1) * CHUNK)
        rm_ref[d, rows, :] = rm
        y0_ref[d, rows, :] = y0
        m_ref[d, rows, :] = m
        n_ref[d, rows, :] = nn


def _rwkv_precompute(rkv, lo, prm, *, n_ctx_rows, lc, ls):
    _, t, _ = rkv.shape
    n_tiles = t // RW_TILE
    halo = RW_TILE // 8
    last_halo = t // 8 - 1
    tile = lambda i, j: (j, i, 0)
    prev = lambda i, j: (j, jnp.maximum(i * halo - 1, 0), 0)
    nxt = lambda i, j: (j, jnp.minimum((i + 1) * halo, last_halo), 0)
    out_dir = jax.ShapeDtypeStruct((2, N_PAIRS, t, LANES), F32)
    out_one = jax.ShapeDtypeStruct((N_PAIRS, t, LANES), F32)
    dir_spec = pl.BlockSpec((2, None, RW_TILE, LANES), lambda i, j: (0, j, i, 0))
    one_spec = pl.BlockSpec((None, RW_TILE, LANES), tile)
    body = functools.partial(_rwkv_pre_body, lat_tiles=ls // RW_TILE, ctx_tiles=lc // RW_TILE,
                             n_ctx_tiles=n_ctx_rows // RW_TILE)
    return pl.pallas_call(
        body,
        grid=(n_tiles, N_PAIRS),
        in_specs=[pl.BlockSpec((None, RW_TILE, 3 * LANES), tile),
                  pl.BlockSpec((None, 8, 3 * LANES), prev),
                  pl.BlockSpec((None, 8, 3 * LANES), nxt),
                  pl.BlockSpec((RW_TILE, 3 * LANES), lambda i, j: (i, 0)),
                  pl.BlockSpec((8, 3 * LANES), lambda i, j: (jnp.maximum(i * halo - 1, 0), 0)),
                  pl.BlockSpec((8, 3 * LANES), lambda i, j: (jnp.minimum((i + 1) * halo, last_halo), 0)),
                  pl.BlockSpec((None, 8, 3 * LANES), lambda i, j: (j, 0, 0)),
                  pl.BlockSpec((8, 3 * LANES), lambda i, j: (0, 0)),
                  pl.BlockSpec((None, 16, LANES), lambda i, j: (j, 0, 0)),
                  pl.BlockSpec((2, None, LANES, LANES), lambda i, j: (0, j, 0, 0)),
                  pl.BlockSpec((2, None, LANES, LANES), lambda i, j: (0, j, 0, 0)),
                  pl.BlockSpec((None, LANES, LANES), lambda i, j: (j, 0, 0))],
        out_specs=[dir_spec, dir_spec, dir_spec, dir_spec, one_spec, one_spec],
        out_shape=[out_dir, out_dir, out_dir, out_dir, out_one, out_one],
        scratch_shapes=[pltpu.VMEM((RW_TILE, LANES), F32), pltpu.VMEM((RW_TILE, LANES), F32),
                        pltpu.VMEM((2, RW_TILE, LANES), F32), pltpu.VMEM((RW_TILE, LANES), F32),
                        pltpu.VMEM((2, RW_TILE, LANES), F32), pltpu.VMEM((2, RW_TILE, LANES), F32)],
        compiler_params=_cparams(("parallel", "parallel")),
        name="rwkv7_chunk_precompute",
    )(rkv, rkv, rkv, lo, lo, lo, prm["mu_rkv"], prm["mu_lo"], prm["vecs"], prm["w2"], prm["a2"], prm["g2"])


def _rwkv_scan_body(rm_c, y0_c, m_c, n_c, rm_l, y0_l, m_l, n_l, bv_c, g_c, bv_l, g_l, vec_ref,
                    oc_ref, ol_ref, yc_s, yl_s):
    def run(rm, y0, mm, nn, y_s, states):
        n_chunks = y_s.shape[1] // CHUNK

        def step(t, st):
            out = []
            for d in range(2):
                c = t if d == 0 else n_chunks - 1 - t
                rows = pl.ds(pl.multiple_of(c * CHUNK, CHUNK), CHUNK)
                both = _dot(jnp.concatenate([_embed(rm[d, rows, :]), _embed(mm[d, rows, :])], axis=0), st[d])
                y_s[d, rows, :] = _fold(both[:LANES]) + y0[d, rows, :]
                out.append(both[LANES:] + _embed(nn[d, rows, :]))
            return tuple(out)

        return lax.fori_loop(0, n_chunks, step, states)

    zero = jnp.zeros((LANES, LANES), F32)
    states = run(rm_c, y0_c, m_c, n_c, yc_s, (zero, zero))
    run(rm_l, y0_l, m_l, n_l, yl_s, states)

    ones = _head_ones()
    gn_w = vec_ref[7:8, :]
    gn_b = vec_ref[8:9, :]

    def finish(y_s, bv, g, o_ref):
        def tile(i, carry):
            rows = pl.ds(pl.multiple_of(i * RW_TILE, RW_TILE), RW_TILE)
            y = y_s[0, rows, :] + y_s[1, rows, :]
            dev = y - _dot_exact_rhs(y, ones, 2) * (1.0 / HEAD_DIM)
            var = _dot_exact_rhs(dev * dev, ones, 2) * (1.0 / HEAD_DIM)
            yn = dev * lax.rsqrt(var + GN_EPS) * gn_w + gn_b
            o_ref[rows, :] = ((yn + bv[rows, :]) * g[rows, :]).astype(o_ref.dtype)
            return carry

        lax.fori_loop(0, y_s.shape[1] // RW_TILE, tile, 0)

    finish(yc_s, bv_c, g_c, oc_ref)
    finish(yl_s, bv_l, g_l, ol_ref)


def _rwkv_scan(rm, y0, mm, nn, bv, g, vecs, *, nb, lc, ls):
    t = rm.shape[2]
    lat0 = nb * lc // ls
    ctx_d = pl.BlockSpec((2, None, lc, LANES), lambda b, j: (0, j, b, 0))
    lat_d = pl.BlockSpec((2, None, ls, LANES), lambda b, j: (0, j, lat0 + b, 0))
    ctx_1 = pl.BlockSpec((None, lc, LANES), lambda b, j: (j, b, 0))
    lat_1 = pl.BlockSpec((None, ls, LANES), lambda b, j: (j, lat0 + b, 0))
    out_c, out_l = pl.pallas_call(
        _rwkv_scan_body,
        grid=(nb, N_PAIRS),
        in_specs=[ctx_d, ctx_d, ctx_d, ctx_d, lat_d, lat_d, lat_d, lat_d, ctx_1, ctx_1, lat_1, lat_1,
                  pl.BlockSpec((None, 16, LANES), lambda b, j: (j, 0, 0))],
        out_specs=[pl.BlockSpec((lc, LANES), lambda b, j: (b, j)),
                   pl.BlockSpec((ls, LANES), lambda b, j: (b, j))],
        out_shape=[jax.ShapeDtypeStruct((nb * lc, GROUP_W), BF16),
                   jax.ShapeDtypeStruct((nb * ls, GROUP_W), BF16)],
        scratch_shapes=[pltpu.VMEM((2, lc, LANES), F32), pltpu.VMEM((2, ls, LANES), F32)],
        compiler_params=_cparams(("parallel", "parallel")),
        name="rwkv7_recurrence_output",
    )(rm, y0, mm, nn, rm, y0, mm, nn, bv, g, bv, g, vecs)
    return jnp.concatenate([out_c, out_l], axis=0)


def _gqa_group_attention(q_pairs, g, k, v, bias, sinks):
    rows = q_pairs[0].shape[0]
    qs = jnp.concatenate([_to_half(q_pairs[p], e, g) for p in range(2) for e in range(2)], axis=0)
    s = _dot_nt(qs, k)
    if bias is not None:
        s = s + jnp.tile(bias, (4, 1))
    sink = None
    if sinks is not None:
        sink = jnp.concatenate([jnp.broadcast_to(sinks[i], (rows, 1)) for i in range(4)], axis=0)
    o = _softmax_pv(s, v, sink)
    return [_merge_heads(o[(2 * p) * rows:(2 * p + 1) * rows], g, o[(2 * p + 1) * rows:(2 * p + 2) * rows], g)
            for p in range(2)]


def _window_body(q_ref, kp_ref, kc_ref, kn_ref, vp_ref, vc_ref, vn_ref, kx_ref, vx_ref,
                 cq_ref, sq_ref, cp_ref, sp_ref, cn_ref, sn_ref, sink_ref, o_ref, *, n_blocks, lc):
    n = pl.program_id(1)
    q = q_ref[...]
    k_lat = jnp.concatenate([_rope(kp_ref[...], cp_ref[...], sp_ref[...]),
                             _rope(kc_ref[...], cq_ref[...], sq_ref[...]),
                             _rope(kn_ref[...], cn_ref[...], sn_ref[...])], axis=0)
    k = jnp.concatenate([kx_ref[...], k_lat], axis=0).astype(BF16)
    v = jnp.concatenate([vx_ref[...], vp_ref[...], vc_ref[...], vn_ref[...]], axis=0).astype(BF16)
    nk = lc + 3 * BLOCK
    qi = lax.broadcasted_iota(jnp.int32, (BLOCK, nk), 0)
    kj = lax.broadcasted_iota(jnp.int32, (BLOCK, nk), 1) - lc
    kblk = n - 1 + kj // BLOCK
    ok = (kj < 0) | ((jnp.abs(kj - BLOCK - qi) <= WINDOW) & (kblk >= 0) & (kblk < n_blocks))
    bias = jnp.where(ok, 0.0, NEG_INF)
    for g in range(2):
        qp = [_rope(q[:, (2 * g + p) * LANES:(2 * g + p + 1) * LANES], cq_ref[...], sq_ref[...]) * ATTN_SCALE
              for p in range(2)]
        sinks = [sink_ref[4 * g + i:4 * g + i + 1, 0:1] for i in range(4)]
        outs = _gqa_group_attention(qp, g, k, v, bias, sinks)
        for p in range(2):
            o_ref[:, (2 * g + p) * LANES:(2 * g + p + 1) * LANES] = outs[p].astype(o_ref.dtype)


def _window_attention(qkv, cos, sin, sink, *, nb, lc, ls):
    n_blocks = ls // BLOCK
    base = nb * lc // BLOCK
    kcol, vcol = GROUP_W // LANES, GROUP_W // LANES + 1
    blk = lambda b, n, off: base + b * n_blocks + jnp.clip(n + off, 0, n_blocks - 1)
    kv_spec = lambda col, off: pl.BlockSpec((BLOCK, LANES), lambda b, n: (blk(b, n, off), col))
    rope_spec = lambda off: pl.BlockSpec((BLOCK, LANES), lambda b, n: (jnp.clip(n + off, 0, n_blocks - 1), 0))
    return pl.pallas_call(
        functools.partial(_window_body, n_blocks=n_blocks, lc=lc),
        grid=(nb, n_blocks),
        in_specs=[pl.BlockSpec((BLOCK, GROUP_W), lambda b, n: (blk(b, n, 0), 0)),
                  kv_spec(kcol, -1), kv_spec(kcol, 0), kv_spec(kcol, 1),
                  kv_spec(vcol, -1), kv_spec(vcol, 0), kv_spec(vcol, 1),
                  pl.BlockSpec((lc, LANES), lambda b, n: (b, kcol)),
                  pl.BlockSpec((lc, LANES), lambda b, n: (b, vcol)),
                  rope_spec(0), rope_spec(0), rope_spec(-1), rope_spec(-1), rope_spec(1), rope_spec(1),
                  pl.BlockSpec((8, LANES), lambda b, n: (0, 0))],
        out_specs=pl.BlockSpec((BLOCK, GROUP_W), lambda b, n: (b * n_blocks + n, 0)),
        out_shape=jax.ShapeDtypeStruct((nb * ls, GROUP_W), BF16),
        compiler_params=_cparams(("parallel", "parallel")),
        name="window_attention",
    )(qkv, qkv, qkv, qkv, qkv, qkv, qkv, qkv, qkv, cos, sin, cos, sin, cos, sin, sink)


def _global_body(q_ref, kx_ref, kl_ref, vx_ref, vl_ref, cq_ref, sq_ref, cos_ref, sin_ref, gq_ref, gk_ref,
                 o_ref, k_scr, v_scr, *, lc):
    @pl.when(pl.program_id(1) == 0)
    def _():
        k_scr[0:lc, :] = _head_rms(kx_ref[...], gk_ref[...]).astype(BF16)
        k_scr[lc:, :] = _rope(_head_rms(kl_ref[...], gk_ref[...]), cos_ref[...], sin_ref[...]).astype(BF16)
        v_scr[0:lc, :] = vx_ref[...].astype(BF16)
        v_scr[lc:, :] = vl_ref[...].astype(BF16)

    q = q_ref[...]
    k = k_scr[...]
    v = v_scr[...]
    for g in range(2):
        qp = [_rope(_head_rms(q[:, (2 * g + p) * LANES:(2 * g + p + 1) * LANES], gq_ref[...]),
                    cq_ref[...], sq_ref[...]) * ATTN_SCALE for p in range(2)]
        outs = _gqa_group_attention(qp, g, k, v, None, None)
        for p in range(2):
            o_ref[:, (2 * g + p) * LANES:(2 * g + p + 1) * LANES] = outs[p].astype(o_ref.dtype)


def _global_attention(qkv, cos, sin, q_gain, k_gain, *, nb, lc, ls):
    n_blocks = ls // BLOCK
    base = nb * lc // BLOCK
    lat0 = nb * lc // ls
    kcol, vcol = GROUP_W // LANES, GROUP_W // LANES + 1
    return pl.pallas_call(
        functools.partial(_global_body, lc=lc),
        grid=(nb, n_blocks),
        in_specs=[pl.BlockSpec((BLOCK, GROUP_W), lambda b, n: (base + b * n_blocks + n, 0)),
                  pl.BlockSpec((lc, LANES), lambda b, n: (b, kcol)),
                  pl.BlockSpec((ls, LANES), lambda b, n: (lat0 + b, kcol)),
                  pl.BlockSpec((lc, LANES), lambda b, n: (b, vcol)),
                  pl.BlockSpec((ls, LANES), lambda b, n: (lat0 + b, vcol)),
                  pl.BlockSpec((BLOCK, LANES), lambda b, n: (n, 0)),
                  pl.BlockSpec((BLOCK, LANES), lambda b, n: (n, 0)),
                  pl.BlockSpec((ls, LANES), lambda b, n: (0, 0)),
                  pl.BlockSpec((ls, LANES), lambda b, n: (0, 0)),
                  pl.BlockSpec((1, LANES), lambda b, n: (0, 0)),
                  pl.BlockSpec((1, LANES), lambda b, n: (0, 0))],
        out_specs=pl.BlockSpec((BLOCK, GROUP_W), lambda b, n: (b * n_blocks + n, 0)),
        out_shape=jax.ShapeDtypeStruct((nb * ls, GROUP_W), BF16),
        scratch_shapes=[pltpu.VMEM((lc + ls, LANES), BF16), pltpu.VMEM((lc + ls, LANES), BF16)],
        compiler_params=_cparams(("parallel", "arbitrary")),
        name="global_attention",
    )(qkv, qkv, qkv, qkv, qkv, cos, sin, cos, sin, q_gain, k_gain)


def _na_body(q_ref, kx_ref, kl_ref, vx_ref, vl_ref, bias_ref, o_ref, *, n_rows, lc):
    r = pl.program_id(1)
    kh = min(NA_KH, n_rows)
    rs = jnp.clip(r - kh // 2, 0, n_rows - kh)
    win = pl.ds(pl.multiple_of(rs * GRID_W, GRID_W), kh * GRID_W)
    q = q_ref[...] * ATTN_SCALE
    for p in range(N_PAIRS):
        cols = slice(p * LANES, (p + 1) * LANES)
        k = jnp.concatenate([kx_ref[:, cols], kl_ref[win, cols]], axis=0).astype(BF16)
        v = jnp.concatenate([vx_ref[:, cols], vl_ref[win, cols]], axis=0).astype(BF16)
        s = _dot_nt(_embed(q[:, cols]), k)
        s = s + jnp.concatenate([jnp.zeros((2 * GRID_W, lc), F32), bias_ref[p]], axis=1)
        o = _softmax_pv(s, v)
        o_ref[:, cols] = jnp.where(_lane_half((GRID_W, LANES)), o[:GRID_W], o[GRID_W:]).astype(o_ref.dtype)


def _na_bias_table(rpb, n_rows):
    kh = min(NA_KH, n_rows)
    qc = np.arange(GRID_W)[:, None]
    kc = np.arange(GRID_W)[None, :]
    win_start = np.clip(qc - NA_KW // 2, 0, GRID_W - NA_KW)
    col_ok = (kc >= win_start) & (kc < win_start + NA_KW)
    dc = np.clip(kc - qc + NA_KW - 1, 0, 2 * NA_KW - 2)
    n_dc = 2 * NA_KW - 1
    heads, n_dr = rpb.shape[0], rpb.shape[1]
    onehot = (dc.reshape(1, -1) == np.arange(n_dc)[:, None]).astype(np.float32)
    by_col = jnp.dot(rpb.reshape(heads * n_dr, n_dc), onehot, precision=lax.Precision.HIGHEST)
    by_col = jnp.where(col_ok[None, None], by_col.reshape(heads, n_dr, GRID_W, GRID_W), NEG_INF)
    tab = jnp.stack([by_col[:, NA_KH - 1 - var:NA_KH - 1 - var + kh] for var in range(kh)])
    tab = tab.transpose(0, 1, 3, 2, 4).reshape(kh, N_PAIRS, 2 * GRID_W, kh * GRID_W)
    return tab.astype(F32)


def _neighbourhood_attention(qkv, bias, *, nb, lc, ls):
    n_rows = ls // GRID_W
    kh = min(NA_KH, n_rows)
    base = nb * lc // GRID_W
    lat0 = nb * lc // ls

    def variant(b, r):
        return (r - jnp.clip(r - kh // 2, 0, n_rows - kh), 0, 0, 0)

    return pl.pallas_call(
        functools.partial(_na_body, n_rows=n_rows, lc=lc),
        grid=(nb, n_rows),
        in_specs=[pl.BlockSpec((GRID_W, GROUP_W), lambda b, r: (base + b * n_rows + r, 0)),
                  pl.BlockSpec((lc, GROUP_W), lambda b, r: (b, 1)),
                  pl.BlockSpec((ls, GROUP_W), lambda b, r: (lat0 + b, 1)),
                  pl.BlockSpec((lc, GROUP_W), lambda b, r: (b, 2)),
                  pl.BlockSpec((ls, GROUP_W), lambda b, r: (lat0 + b, 2)),
                  pl.BlockSpec((None, N_PAIRS, 2 * GRID_W, kh * GRID_W), variant)],
        out_specs=pl.BlockSpec((GRID_W, GROUP_W), lambda b, r: (b * n_rows + r, 0)),
        out_shape=jax.ShapeDtypeStruct((nb * ls, GROUP_W), BF16),
        compiler_params=_cparams(("parallel", "parallel")),
        name="neighbourhood_attention",
    )(qkv, qkv, qkv, qkv, qkv, bias)


def _ctx_body(q_ref, k_ref, v_ref, sink_ref, gq_ref, gk_ref, o_ref, *, gqa, use_sink, use_norm):
    q = q_ref[...]
    k = k_ref[...]
    v = v_ref[...].astype(BF16)
    if gqa:
        if use_norm:
            k = _head_rms(k, gk_ref[...])
        k = k.astype(BF16)
        for g in range(2):
            qp = []
            for p in range(2):
                x = q[:, (2 * g + p) * LANES:(2 * g + p + 1) * LANES]
                if use_norm:
                    x = _head_rms(x, gq_ref[...])
                qp.append(x * ATTN_SCALE)
            sinks = [sink_ref[4 * g + i:4 * g + i + 1, 0:1] for i in range(4)] if use_sink else None
            outs = _gqa_group_attention(qp, g, k, v, None, sinks)
            for p in range(2):
                o_ref[:, (2 * g + p) * LANES:(2 * g + p + 1) * LANES] = outs[p].astype(o_ref.dtype)
    else:
        rows = q.shape[0]
        for p in range(N_PAIRS):
            cols = slice(p * LANES, (p + 1) * LANES)
            o = _softmax_pv(_dot_nt(_embed(q[:, cols] * ATTN_SCALE), k[:, cols]), v[:, cols])
            o_ref[:, cols] = jnp.where(_lane_half((rows, LANES)), o[:rows], o[rows:]).astype(o_ref.dtype)


def _ctx_attention(qkv, *, nb, lc, gqa, q_col=0, sink=None, q_gain=None, k_gain=None):
    kw = KV_W if gqa else GROUP_W
    kcol = (q_col * GROUP_W + GROUP_W) // kw
    dummy = jnp.zeros((8, LANES), F32)
    one = jnp.ones((1, LANES), F32)
    return pl.pallas_call(
        functools.partial(_ctx_body, gqa=gqa, use_sink=sink is not None, use_norm=q_gain is not None),
        grid=(nb,),
        in_specs=[pl.BlockSpec((lc, GROUP_W), lambda b: (b, q_col)),
                  pl.BlockSpec((lc, kw), lambda b: (b, kcol)),
                  pl.BlockSpec((lc, kw), lambda b: (b, kcol + 1)),
                  pl.BlockSpec((8, LANES), lambda b: (0, 0)),
                  pl.BlockSpec((1, LANES), lambda b: (0, 0)),
                  pl.BlockSpec((1, LANES), lambda b: (0, 0))],
        out_specs=pl.BlockSpec((lc, GROUP_W), lambda b: (b, 0)),
        out_shape=jax.ShapeDtypeStruct((nb * lc, GROUP_W), BF16),
        compiler_params=_cparams(("parallel",)),
        name="context_attention",
    )(qkv, qkv, qkv, dummy if sink is None else sink, one if q_gain is None else q_gain,
      one if k_gain is None else k_gain)


def _rope_tables(ls):
    t = jnp.arange(ls, dtype=jnp.int32)
    n_freq = HEAD_DIM // 4
    inv_freq = ROPE_BASE ** (-jnp.arange(n_freq, dtype=F32) / n_freq)
    row = (t // GRID_W).astype(F32)[:, None] * inv_freq
    col = (t % GRID_W).astype(F32)[:, None] * inv_freq
    cos = jnp.concatenate([jnp.cos(row), jnp.cos(row), jnp.cos(col), jnp.cos(col)], axis=-1)
    sin = jnp.concatenate([-jnp.sin(row), jnp.sin(row), -jnp.sin(col), jnp.sin(col)], axis=-1)
    return jnp.tile(cos, (1, 2)), jnp.tile(sin, (1, 2))


def _rwkv_params(mu_prev, mu_next, w0, w2, a0, a2, g2, k_k, k_a, r_k, gn_w, gn_b):
    aw = GROUP_W
    pair = lambda x: x.reshape(N_PAIRS, LANES)

    def mu_rkv(mu):
        return jnp.concatenate([pair(mu[0:aw]), pair(mu[aw:2 * aw]), pair(mu[2 * aw:3 * aw])], axis=-1)

    mu_a = jnp.zeros((N_PAIRS, 8, 3 * LANES), F32)
    mu_a = mu_a.at[:, 0].set(mu_rkv(mu_prev)).at[:, 1].set(mu_rkv(mu_next))
    mu_l = jnp.zeros((8, 3 * LANES), F32).at[0].set(mu_prev[3 * aw:]).at[1].set(mu_next[3 * aw:])
    vecs = jnp.zeros((N_PAIRS, 16, LANES), F32)
    rows = [pair(w0[0]), pair(w0[1]), pair(a0[0]), pair(a0[1]), pair(k_k), pair(k_a), pair(r_k.reshape(-1)),
            pair(gn_w), pair(gn_b)]
    for i, val in enumerate(rows):
        vecs = vecs.at[:, i].set(val)

    def lora_pad(w):
        out = jnp.zeros((2, N_PAIRS, LANES, LANES), F32)
        for d in range(2):
            blk = w[d].reshape(DECAY_LORA, N_PAIRS, LANES).transpose(1, 0, 2)
            out = out.at[d, :, d * DECAY_LORA:(d + 1) * DECAY_LORA, :].set(blk)
        return out

    g2p = g2.reshape(GATE_LORA, N_PAIRS, LANES).transpose(1, 0, 2)
    return dict(mu_rkv=mu_a, mu_lo=mu_l, vecs=vecs, w2=lora_pad(w2), a2=lora_pad(a2), g2=g2p)


def _lane_bcast(x, rows):
    return jnp.zeros((rows, LANES), F32).at[:x.shape[0]].set(jnp.broadcast_to(x[:, None], (x.shape[0], LANES)))


A_W = GROUP_W
A_IN = 3 * A_W + 3 * LANES
B_IN = GROUP_W + 2 * KV_W


def _even_mixer(h, mods_l, gain, w_in, a_prm, sink, cos, sin, *, dims):
    nb, lc, ls = dims["nb"], dims["lc"], dims["ls"]
    t = h.shape[0]
    rkv_dests = [(0, j, part * LANES, part * A_W + j * LANES, LANES)
                 for j in range(N_PAIRS) for part in range(3)]
    pieces = [(0, 3 * A_W, rkv_dests),
              (3 * A_W, 3 * LANES, [(1, None, 0, 0, 3 * LANES)]),
              (A_IN, B_IN, [(2, None, 0, 0, B_IN)])]
    rkv, lo, qkv = _in_proj(h, mods_l, gain, w_in, [(N_PAIRS, t, 3 * LANES), (t, 3 * LANES), (t, B_IN)], pieces,
                            tm=dims["tm"], n_ctx_rows=nb * lc, ls=ls, nb=nb)
    rm, y0, mm, nn, bv, g = _rwkv_precompute(rkv, lo, a_prm, n_ctx_rows=nb * lc, lc=lc, ls=ls)
    mix_a = _rwkv_scan(rm, y0, mm, nn, bv, g, a_prm["vecs"], nb=nb, lc=lc, ls=ls)
    sink_b = _lane_bcast(sink, 8)
    b_lat = _window_attention(qkv, cos, sin, sink_b, nb=nb, lc=lc, ls=ls)
    b_ctx = _ctx_attention(qkv, nb=nb, lc=lc, gqa=True, sink=sink_b)
    return mix_a, jnp.concatenate([b_ctx, b_lat], axis=0)


def _odd_mixer(h, mods_l, gain, w_in, rpb, q_gain, k_gain, cos, sin, *, dims):
    nb, lc, ls = dims["nb"], dims["lc"], dims["ls"]
    t = h.shape[0]
    c_in = 3 * GROUP_W
    pieces = [(0, c_in, [(0, None, 0, 0, c_in)]), (c_in, B_IN, [(1, None, 0, 0, B_IN)])]
    qkv_c, qkv_d = _in_proj(h, mods_l, gain, w_in, [(t, c_in), (t, B_IN)], pieces,
                            tm=dims["tm"], n_ctx_rows=nb * lc, ls=ls, nb=nb)
    bias = _na_bias_table(rpb, ls // GRID_W)
    c_lat = _neighbourhood_attention(qkv_c, bias, nb=nb, lc=lc, ls=ls)
    c_ctx = _ctx_attention(qkv_c, nb=nb, lc=lc, gqa=False)
    gq = jnp.tile(q_gain, 2)[None, :]
    gk = jnp.tile(k_gain, 2)[None, :]
    d_lat = _global_attention(qkv_d, cos, sin, gq, gk, nb=nb, lc=lc, ls=ls)
    d_ctx = _ctx_attention(qkv_d, nb=nb, lc=lc, gqa=True, q_gain=gq, k_gain=gk)
    return jnp.concatenate([c_ctx, c_lat], axis=0), jnp.concatenate([d_ctx, d_lat], axis=0)


def _row_tile(target, rows_ctx, ls):
    tile = target
    while rows_ctx % tile or ls % tile:
        tile //= 2
    return tile


def kernel(x, c, ctx, c_ctx, w_ada, b_ada, g_pre_mix, g_post_mix, g_pre_ff, g_post_ff, w_in_even, w_in_odd, w_out, w_ff1, w_ff2, a_mu_prev, a_mu_next, a_w0, a_w2, a_a0, a_a2, a_g2, a_k_k, a_k_a, a_r_k, a_gn_w, a_gn_b, b_sink, c_rpb, d_q_gain, d_k_gain):
    nb, ls, d = x.shape
    lc = ctx.shape[1]
    depth = w_ada.shape[0]
    assert nb < 16 and lc % RW_TILE == 0 and ls % RW_TILE == 0 and (nb * lc) % ls == 0
    rows_ctx = nb * lc
    dims = dict(nb=nb, lc=lc, ls=ls, tm=_row_tile(512, rows_ctx, ls), tm_mlp=_row_tile(1024, rows_ctx, ls))
    n_ctx_rows = nb * lc
    cvec = jnp.zeros((16, d), F32).at[:nb].set(c).at[nb].set(c_ctx)
    mods = _modulation(cvec, w_ada, b_ada)
    cos, sin = _rope_tables(ls)
    h = jnp.concatenate([ctx.reshape(n_ctx_rows, d), x.reshape(nb * ls, d)], axis=0)
    for i in range(depth):
        j = i // 2
        last = i == depth - 1
        if i % 2 == 0:
            a_prm = _rwkv_params(a_mu_prev[j], a_mu_next[j], a_w0[j], a_w2[j], a_a0[j], a_a2[j], a_g2[j],
                                 a_k_k[j], a_k_a[j], a_r_k[j], a_gn_w[j], a_gn_b[j])
            mix_a, mix_b = _even_mixer(h, mods[i], g_pre_mix[i], w_in_even[j].astype(BF16), a_prm, b_sink[j],
                                       cos, sin, dims=dims)
        else:
            mix_a, mix_b = _odd_mixer(h, mods[i], g_pre_mix[i], w_in_odd[j].astype(BF16), c_rpb[j],
                                      d_q_gain[j], d_k_gain[j], cos, sin, dims=dims)
        h = _out_proj(mix_a, mix_b, h, mods[i], g_post_mix[i], w_out[i].astype(BF16),
                      tm=dims["tm"], n_ctx_rows=n_ctx_rows, ls=ls, nb=nb)
        h = _mlp(h, mods[i], g_pre_ff[i], g_post_ff[i], w_ff1[i], w_ff2[i],
                 tm=dims["tm_mlp"], fc=512, n_ctx_rows=n_ctx_rows, ls=ls, nb=nb,
                 first_row=n_ctx_rows if last else 0)
    return h.reshape(nb, ls, d)
```

```python
import functools

import numpy as np
import jax
import jax.numpy as jnp
from jax import lax
from jax.experimental import pallas as pl
from jax.experimental.pallas import tpu as pltpu

F32 = jnp.float32
BF16 = jnp.bfloat16

HEAD_DIM = 64
LANES = 128
N_PAIRS = 4
GROUP_W = N_PAIRS * LANES
KV_W = LANES
DECAY_LORA = 64
GATE_LORA = 128
WINDOW = 128
BLOCK = 128
GRID_W = 64
NA_KH = 8
NA_KW = 16
ROPE_BASE = 10000.0
NORM_EPS = 1e-6
GN_EPS = 64e-5
NEG_INF = -1e30
ATTN_SCALE = HEAD_DIM ** -0.5
LOG2E = 1.4426950408889634
Q_SCALE = ATTN_SCALE * LOG2E
CHUNK = 64
RW_TILE = 256
VMEM_LIMIT = 56 * 1024 * 1024


def _cparams(sem):
    return pltpu.CompilerParams(dimension_semantics=sem, vmem_limit_bytes=VMEM_LIMIT)


def _dot(a, b):
    return jnp.dot(a.astype(BF16), b.astype(BF16), preferred_element_type=F32)


def _dot_nt(a, b):
    return lax.dot_general(a.astype(BF16), b.astype(BF16), (((1,), (1,)), ((), ())),
                           preferred_element_type=F32)


def _dot_tn(a, b):
    return lax.dot_general(a.astype(BF16), b.astype(BF16), (((0,), (0,)), ((), ())),
                           preferred_element_type=F32)


def _dot_f32(a, b):
    return jnp.dot(a, b, preferred_element_type=F32, precision=lax.Precision.HIGHEST)


def _dot_exact_rhs(x, b, pieces):
    b = b.astype(BF16)
    acc = None
    for _ in range(pieces):
        part = x.astype(BF16)
        term = jnp.dot(part, b, preferred_element_type=F32)
        acc = term if acc is None else acc + term
        x = x - part.astype(F32)
    return acc


def _dot_exact_lhs(a, x, pieces):
    a = a.astype(BF16)
    acc = None
    for _ in range(pieces):
        part = x.astype(BF16)
        term = jnp.dot(a, part, preferred_element_type=F32)
        acc = term if acc is None else acc + term
        x = x - part.astype(F32)
    return acc


def _lane_half(shape):
    return lax.broadcasted_iota(jnp.int32, shape, len(shape) - 1) % LANES < HEAD_DIM


def _head_ones():
    r = lax.broadcasted_iota(jnp.int32, (LANES, LANES), 0) // HEAD_DIM
    c = lax.broadcasted_iota(jnp.int32, (LANES, LANES), 1) // HEAD_DIM
    return (r == c).astype(F32)


def _embed(x):
    first = _lane_half(x.shape)
    zero = jnp.zeros_like(x)
    return jnp.concatenate([jnp.where(first, x, zero), jnp.where(first, zero, x)], axis=0)


def _fold(x):
    n = x.shape[0] // 2
    return x[:n] + x[n:]


def _sigmoid(x):
    return 1.0 / (1.0 + jnp.exp(-x))


def _rms_rows(x, gain):
    return x * lax.rsqrt(jnp.mean(x * x, axis=-1, keepdims=True) + NORM_EPS) * gain


def _rope(x, cos, sin):
    lane = lax.broadcasted_iota(jnp.int32, x.shape, 1)
    swapped = jnp.where(lane % 32 < 16, pltpu.roll(x, LANES - 16, 1), pltpu.roll(x, 16, 1))
    return x * cos + swapped * sin


def _head_rms(x, gain):
    ms = _dot_exact_rhs(x * x, _head_ones(), 2) * (1.0 / HEAD_DIM)
    return x * lax.rsqrt(ms + NORM_EPS) * gain


def _to_half(x, src, dst):
    if src != dst:
        x = pltpu.roll(x, HEAD_DIM, 1)
    first = _lane_half(x.shape)
    keep = first if dst == 0 else jnp.logical_not(first)
    return jnp.where(keep, x, jnp.zeros_like(x))


def _softmax_pv(s, v, sink=None):
    m = jnp.max(s, axis=-1, keepdims=True)
    if sink is not None:
        m = jnp.maximum(m, sink)
    p = jnp.exp2(s - m)
    den = jnp.sum(p, axis=-1, keepdims=True)
    if sink is not None:
        den = den + jnp.exp2(sink - m)
    return _dot(p, v) / den


def _merge_heads(o0, src0, o1, src1):
    a = o0 if src0 == 0 else pltpu.roll(o0, HEAD_DIM, 1)
    b = o1 if src1 == 1 else pltpu.roll(o1, HEAD_DIM, 1)
    return jnp.where(_lane_half(a.shape), a, b)


def _mod_body(c_ref, w_ref, b_ref, o_ref):
    c = c_ref[...]
    s = c * _sigmoid(c)
    o_ref[...] = _dot_f32(s, w_ref[...]) + b_ref[...]


def _modulation(cvec, w_ada, b_ada):
    depth, d, _ = w_ada.shape
    out = pl.pallas_call(
        _mod_body,
        grid=(depth, 6),
        in_specs=[pl.BlockSpec((16, d), lambda l, n: (0, 0)),
                  pl.BlockSpec((None, d, d), lambda l, n: (l, 0, n)),
                  pl.BlockSpec((None, None, 1, d), lambda l, n: (l, n, 0, 0))],
        out_specs=pl.BlockSpec((None, None, 16, d), lambda l, n: (l, n, 0, 0)),
        out_shape=jax.ShapeDtypeStruct((depth, 6, 16, d), F32),
        compiler_params=_cparams(("parallel", "parallel")),
        name="adaln_modulation",
    )(cvec, w_ada, b_ada.reshape(depth, 6, 1, d))
    return out.transpose(0, 2, 1, 3)


def _mod_row(tile, tm, n_ctx_rows, ls, nb):
    start = tile * tm
    return jnp.where(start < n_ctx_rows, nb, (start - n_ctx_rows) // ls)


def _row_specs(arrs, tm, n_ctx_tiles):
    width = arrs[0].shape[1]
    if len(arrs) == 1:
        return [pl.BlockSpec((tm, width), lambda i: (i, 0))]
    return [pl.BlockSpec((tm, width), lambda i: (jnp.minimum(i, n_ctx_tiles - 1), 0)),
            pl.BlockSpec((tm, width), lambda i: (jnp.maximum(i - n_ctx_tiles, 0), 0))]


def _read_rows(refs, n_ctx_tiles):
    if len(refs) == 1:
        return refs[0][...]
    return jnp.where(pl.program_id(0) < n_ctx_tiles, refs[0][...], refs[1][...])


def _inproj_body(*refs, pieces, n_h, n_ctx_tiles):
    x = _read_rows(refs[:n_h], n_ctx_tiles)
    m_ref, g_ref, w_ref = refs[n_h:n_h + 3]
    o_refs = refs[n_h + 3:]
    u = _rms_rows(x, g_ref[...]) * (1.0 + m_ref[1:2, :]) + m_ref[0:1, :]
    u = u.astype(BF16)
    for (src, span, dests) in pieces:
        val = jnp.dot(u, w_ref[:, src:src + span], preferred_element_type=F32)
        for (oi, lead, dst, off, width) in dests:
            if lead is None:
                o_refs[oi][:, dst:dst + width] = val[:, off:off + width]
            else:
                o_refs[oi][lead, :, dst:dst + width] = val[:, off:off + width]


def _in_proj(h, mods, gains, w, layer, w_layer, out_defs, pieces, *, tm, n_ctx_rows, ls, nb):
    t = sum(a.shape[0] for a in h)
    d = h[0].shape[1]
    n_in = w.shape[2]
    nct = n_ctx_rows // tm
    out_shapes, out_specs = [], []
    for shape in out_defs:
        out_shapes.append(jax.ShapeDtypeStruct(shape, F32))
        if len(shape) == 3:
            out_specs.append(pl.BlockSpec((shape[0], tm, shape[2]), lambda i: (0, i, 0)))
        else:
            out_specs.append(pl.BlockSpec((tm, shape[1]), lambda i: (i, 0)))
    row = functools.partial(_mod_row, tm=tm, n_ctx_rows=n_ctx_rows, ls=ls, nb=nb)
    return pl.pallas_call(
        functools.partial(_inproj_body, pieces=pieces, n_h=len(h), n_ctx_tiles=nct),
        grid=(t // tm,),
        in_specs=_row_specs(h, tm, nct) + [
            pl.BlockSpec((None, None, 6, d), lambda i: (layer, row(i), 0, 0)),
            pl.BlockSpec((None, 1, d), lambda i: (layer, 0, 0)),
            pl.BlockSpec((None, d, n_in), lambda i: (w_layer, 0, 0))],
        out_specs=out_specs,
        out_shape=out_shapes,
        compiler_params=_cparams(("parallel",)),
        name="modulate_in_proj",
    )(*h, mods, gains, w)


def _outproj_body(*refs, n_h, n_ctx_tiles):
    a = _read_rows(refs[0:2], n_ctx_tiles)
    b = _read_rows(refs[2:4], n_ctx_tiles)
    x = _read_rows(refs[4:4 + n_h], n_ctx_tiles)
    m_ref, g_ref, w_ref, o_ref = refs[4 + n_h:]
    o = (jnp.dot(a, w_ref[0:GROUP_W, :], preferred_element_type=F32)
         + jnp.dot(b, w_ref[GROUP_W:2 * GROUP_W, :], preferred_element_type=F32))
    o_ref[...] = x + m_ref[2:3, :] * _rms_rows(o, g_ref[...])


def _out_proj(mix_a, mix_b, h, mods, gains, w, layer, *, tm, n_ctx_rows, ls, nb):
    t = sum(a.shape[0] for a in h)
    d = h[0].shape[1]
    nct = n_ctx_rows // tm
    row = functools.partial(_mod_row, tm=tm, n_ctx_rows=n_ctx_rows, ls=ls, nb=nb)
    return pl.pallas_call(
        functools.partial(_outproj_body, n_h=len(h), n_ctx_tiles=nct),
        grid=(t // tm,),
        in_specs=_row_specs(mix_a, tm, nct) + _row_specs(mix_b, tm, nct) + _row_specs(h, tm, nct) + [
            pl.BlockSpec((None, None, 6, d), lambda i: (layer, row(i), 0, 0)),
            pl.BlockSpec((None, 1, d), lambda i: (layer, 0, 0)),
            pl.BlockSpec((None, 2 * GROUP_W, d), lambda i: (layer, 0, 0))],
        out_specs=pl.BlockSpec((tm, d), lambda i: (i, 0)),
        out_shape=jax.ShapeDtypeStruct((t, d), F32),
        compiler_params=_cparams(("parallel",)),
        name="out_proj_gate_residual",
    )(*mix_a, *mix_b, *h, mods, gains, w)


def _mlp_body(x_ref, m_ref, g1_ref, g2_ref, w1_ref, w2_ref, o_ref, *, n_split):
    x = x_ref[...]
    u = (_rms_rows(x, g1_ref[...]) * (1.0 + m_ref[4:5, :]) + m_ref[3:4, :]).astype(BF16)
    fc = w1_ref.shape[1] // n_split
    acc = None
    for c in range(n_split):
        a = jnp.dot(u, w1_ref[:, c * fc:(c + 1) * fc], preferred_element_type=F32)
        a = jnp.square(jnp.maximum(a, 0.0)).astype(BF16)
        part = jnp.dot(a, w2_ref[c * fc:(c + 1) * fc, :], preferred_element_type=F32)
        acc = part if acc is None else acc + part
    o_ref[...] = x + m_ref[5:6, :] * _rms_rows(acc, g2_ref[...])


def _mlp(h, mods, g_pre, g_post, w1, w2, layer, *, tm, n_split, n_ctx_rows, ls, nb, first_row):
    t, d = h.shape
    dff = w1.shape[2]
    off = first_row // tm
    row = functools.partial(_mod_row, tm=tm, n_ctx_rows=n_ctx_rows, ls=ls, nb=nb)
    once = pl.Buffered(1)
    return pl.pallas_call(
        functools.partial(_mlp_body, n_split=n_split),
        grid=((t - first_row) // tm,),
        in_specs=[pl.BlockSpec((tm, d), lambda i: (i + off, 0)),
                  pl.BlockSpec((None, None, 6, d), lambda i: (layer, row(i + off), 0, 0)),
                  pl.BlockSpec((None, 1, d), lambda i: (layer, 0, 0)),
                  pl.BlockSpec((None, 1, d), lambda i: (layer, 0, 0)),
                  pl.BlockSpec((None, d, dff), lambda i: (layer, 0, 0), pipeline_mode=once),
                  pl.BlockSpec((None, dff, d), lambda i: (layer, 0, 0), pipeline_mode=once)],
        out_specs=pl.BlockSpec((tm, d), lambda i: (i, 0)),
        out_shape=jax.ShapeDtypeStruct((t - first_row, d), F32),
        compiler_params=_cparams(("parallel",)),
        name="sqrelu_mlp_gate_residual",
    )(h, mods, g_pre, g_post, w1, w2)


def _rwkv_units(units):
    n = units[0][0].shape[0]
    n2 = 2 * n
    rev = [u[6] for u in units]
    ri = lax.broadcasted_iota(jnp.int32, (n, n), 0)
    ci = lax.broadcasted_iota(jnp.int32, (n, n), 1)
    tri = {False: (ci <= ri).astype(F32), True: (ci >= ri).astype(F32)}
    rr = lax.broadcasted_iota(jnp.int32, (n2, n2), 0)
    cc = lax.broadcasted_iota(jnp.int32, (n2, n2), 1)
    r2, c2 = rr % n, cc % n
    incl2 = {False: c2 <= r2, True: c2 >= r2}
    strict2 = {False: c2 < r2, True: c2 > r2}
    eye = rr == cc
    eye_f = jnp.where(eye, 1.0, 0.0)
    zero = jnp.zeros((n2, n2), F32)
    base = 8
    diag_blk = rr // base == cc // base
    sizes = []
    size = base
    while size < n:
        sizes.append(size)
        size *= 2
    off_blk = [(rr // (2 * s) == cc // (2 * s)) & (rr // s != cc // s) for s in sizes]

    cs = [_dot_exact_lhs(tri[u[6]], u[5], 3) for u in units]
    ops = []
    for (r, v, kd, alpha, beta, logw, _), c in zip(units, cs):
        c_all = jnp.sum(logw, axis=0, keepdims=True)
        e_pos = jnp.exp(c)
        e_neg = jnp.exp(-c)
        e_end = jnp.exp(c_all - c)
        ops.append(dict(rt=_embed(r * e_pos), at=_embed(alpha * jnp.exp(c - logw)), kt=_embed(kd * e_neg),
                        bt=_embed(beta * e_neg), kh=_embed(kd * e_end), bh=_embed(beta * e_end), vb=_embed(v),
                        decay=jnp.exp(c_all)))
    a_all = [_dot_nt(jnp.concatenate([o["rt"], o["at"]], axis=0), jnp.concatenate([o["kt"], o["bt"]], axis=0))
             for o in ops]
    a_qk = [jnp.where(incl2[f], a[:n2, :n2], zero) for a, f in zip(a_all, rev)]
    a_qb = [jnp.where(incl2[f], a[:n2, n2:], zero) for a, f in zip(a_all, rev)]
    a_ak = [jnp.where(strict2[f], a[n2:, :n2], zero) for a, f in zip(a_all, rev)]
    a_ab = [jnp.where(strict2[f], a[n2:, n2:], zero) for a, f in zip(a_all, rev)]
    akv = [_dot(a, o["vb"]) for a, o in zip(a_ak, ops)]
    apow = [jnp.where(diag_blk, a, zero) for a in a_ab]
    inv = [eye_f + a for a in apow]
    for _ in range(2):
        apow = [_dot(a, a) for a in apow]
        inv = [x + _dot(x, a) for x, a in zip(inv, apow)]
    for blk in off_blk:
        right = [_dot(jnp.where(blk, a, zero), x) for a, x in zip(a_ab, inv)]
        inv = [x + _dot(x, y) for x, y in zip(inv, right)]
    wu = [_dot(x, jnp.concatenate([o["at"], y], axis=1)) for x, o, y in zip(inv, ops, akv)]
    outs = []
    for o, x, qb, qk in zip(ops, wu, a_qb, a_qk):
        w = x[:, :LANES]
        u0v = jnp.concatenate([x[:, LANES:], o["vb"]], axis=0)
        m = jnp.where(eye, o["decay"], 0.0) + _dot_tn(o["bh"], w)
        nn = _dot_tn(jnp.concatenate([o["bh"], o["kh"]], axis=0), u0v)
        rm = o["rt"] + _dot(qb, w)
        y0 = _dot(jnp.concatenate([qb, qk], axis=1), u0v)
        outs.append((_fold(rm), _fold(y0), _fold(m), _fold(nn)))
    return outs


def _shifted(x, prev_row, next_row, mu_prev, mu_next):
    rows = x.shape[0]
    ridx = lax.broadcasted_iota(jnp.int32, x.shape, 0)
    prev = jnp.where(ridx == 0, prev_row, pltpu.roll(x, 1, 0))
    nxt = jnp.where(ridx == rows - 1, next_row, pltpu.roll(x, rows - 1, 0))
    return x + mu_prev * (prev - x) + mu_next * (nxt - x)


def _rwkv_pre_body(rkv_ref, rkv_p_ref, rkv_n_ref, lo_ref, lo_p_ref, lo_n_ref,
                   mu_rkv_ref, mu_lo_ref, vec_ref, w2_ref, a2_ref, g2_ref,
                   rm_ref, y0_ref, m_ref, n_ref, bv_ref, g_ref,
                   r_s, v_s, kd_s, al_s, be_s, lw_s, *, lat_tiles, ctx_tiles, n_ctx_tiles):
    i = pl.program_id(0)
    lat = i >= n_ctx_tiles
    pos = jnp.where(lat, (i - n_ctx_tiles) % lat_tiles, i % ctx_tiles)
    has_prev = (pos != 0).astype(F32)
    has_next = (pos != jnp.where(lat, lat_tiles, ctx_tiles) - 1).astype(F32)
    rkv = _shifted(rkv_ref[...], rkv_p_ref[7:8, :] * has_prev, rkv_n_ref[0:1, :] * has_next,
                   mu_rkv_ref[0:1, :], mu_rkv_ref[1:2, :])
    lo = _shifted(lo_ref[...], lo_p_ref[7:8, :] * has_prev, lo_n_ref[0:1, :] * has_next,
                  mu_lo_ref[0:1, :], mu_lo_ref[1:2, :])
    r = rkv[:, 0:LANES]
    k = rkv[:, LANES:2 * LANES]
    v = rkv[:, 2 * LANES:3 * LANES]
    wl = jnp.tanh(lo[:, 0:LANES])
    al = lo[:, LANES:2 * LANES]
    gl = _sigmoid(lo[:, 2 * LANES:3 * LANES])
    ones = _head_ones()
    kx = k * vec_ref[4:5, :]
    kk = kx / jnp.maximum(jnp.sqrt(_dot_exact_rhs(kx * kx, ones, 2)), 1e-12)
    r_s[...] = r
    v_s[...] = v
    al_s[...] = -kk
    ksum = jnp.zeros_like(k)
    for d in range(2):
        w_raw = vec_ref[d:d + 1, :] + _dot(wl, w2_ref[d])
        z = -w_raw
        softplus = jnp.maximum(z, 0.0) + jnp.log(1.0 + jnp.exp(-jnp.abs(z)))
        lw_s[d] = -jnp.exp(-softplus - 0.5)
        iclr = _sigmoid(vec_ref[2 + d:3 + d, :] + _dot(al, a2_ref[d]))
        kd = k * (1.0 + (iclr - 1.0) * vec_ref[5:6, :])
        kd_s[d] = kd
        be_s[d] = kk * iclr
        ksum = ksum + kd
    bonus = _dot_exact_rhs(r * ksum * vec_ref[6:7, :], ones, 2)
    bv_ref[...] = bonus * v
    g_ref[...] = _dot(gl, g2_ref[...])

    units = []
    for ci in range(RW_TILE // CHUNK):
        rows = slice(ci * CHUNK, (ci + 1) * CHUNK)
        for d in range(2):
            units.append((r_s[rows, :], v_s[rows, :], kd_s[d, rows, :], al_s[rows, :], be_s[d, rows, :],
                          lw_s[d, rows, :], d == 1))
    outs = _rwkv_units(units)
    for idx, (rm, y0, m, nn) in enumerate(outs):
        ci, d = divmod(idx, 2)
        rows = slice(ci * CHUNK, (ci + 1) * CHUNK)
        rm_ref[d, rows, :] = rm.astype(BF16)
        y0_ref[d, rows, :] = y0
        m_ref[d, rows, :] = m.astype(BF16)
        n_ref[d, rows, :] = nn


def _rwkv_precompute(rkv, lo, prm, *, n_ctx_rows, lc, ls):
    _, t, _ = rkv.shape
    n_tiles = t // RW_TILE
    halo = RW_TILE // 8
    last_halo = t // 8 - 1
    tile = lambda i, j: (j, i, 0)
    prev = lambda i, j: (j, jnp.maximum(i * halo - 1, 0), 0)
    nxt = lambda i, j: (j, jnp.minimum((i + 1) * halo, last_halo), 0)
    out_dir = jax.ShapeDtypeStruct((2, N_PAIRS, t, LANES), F32)
    out_mat = jax.ShapeDtypeStruct((2, N_PAIRS, t, LANES), BF16)
    out_one = jax.ShapeDtypeStruct((N_PAIRS, t, LANES), F32)
    dir_spec = pl.BlockSpec((2, None, RW_TILE, LANES), lambda i, j: (0, j, i, 0))
    one_spec = pl.BlockSpec((None, RW_TILE, LANES), tile)
    body = functools.partial(_rwkv_pre_body, lat_tiles=ls // RW_TILE, ctx_tiles=lc // RW_TILE,
                             n_ctx_tiles=n_ctx_rows // RW_TILE)
    return pl.pallas_call(
        body,
        grid=(n_tiles, N_PAIRS),
        in_specs=[pl.BlockSpec((None, RW_TILE, 3 * LANES), tile),
                  pl.BlockSpec((None, 8, 3 * LANES), prev),
                  pl.BlockSpec((None, 8, 3 * LANES), nxt),
                  pl.BlockSpec((RW_TILE, 3 * LANES), lambda i, j: (i, 0)),
                  pl.BlockSpec((8, 3 * LANES), lambda i, j: (jnp.maximum(i * halo - 1, 0), 0)),
                  pl.BlockSpec((8, 3 * LANES), lambda i, j: (jnp.minimum((i + 1) * halo, last_halo), 0)),
                  pl.BlockSpec((None, 8, 3 * LANES), lambda i, j: (j, 0, 0)),
                  pl.BlockSpec((8, 3 * LANES), lambda i, j: (0, 0)),
                  pl.BlockSpec((None, 16, LANES), lambda i, j: (j, 0, 0)),
                  pl.BlockSpec((2, None, LANES, LANES), lambda i, j: (0, j, 0, 0)),
                  pl.BlockSpec((2, None, LANES, LANES), lambda i, j: (0, j, 0, 0)),
                  pl.BlockSpec((None, LANES, LANES), lambda i, j: (j, 0, 0))],
        out_specs=[dir_spec, dir_spec, dir_spec, dir_spec, one_spec, one_spec],
        out_shape=[out_mat, out_dir, out_mat, out_dir, out_one, out_one],
        scratch_shapes=[pltpu.VMEM((RW_TILE, LANES), F32), pltpu.VMEM((RW_TILE, LANES), F32),
                        pltpu.VMEM((2, RW_TILE, LANES), F32), pltpu.VMEM((RW_TILE, LANES), F32),
                        pltpu.VMEM((2, RW_TILE, LANES), F32), pltpu.VMEM((2, RW_TILE, LANES), F32)],
        compiler_params=_cparams(("parallel", "parallel")),
        name="rwkv7_chunk_precompute",
    )(rkv, rkv, rkv, lo, lo, lo, prm["mu_rkv"], prm["mu_lo"], prm["vecs"], prm["w2"], prm["a2"], prm["g2"])


SCAN_PAIRS = 2


def _rwkv_scan_body(rm_c, y0_c, m_c, n_c, rm_l, y0_l, m_l, n_l, bv_c, g_c, bv_l, g_l, vec_ref,
                    oc_ref, ol_ref, yc_s, yl_s):
    chains = [(d, p) for d in range(2) for p in range(SCAN_PAIRS)]

    def run(rm, y0, mm, nn, y_s, states):
        n_chunks = y_s.shape[2] // CHUNK

        def step(t, st):
            out = []
            for (d, p), s in zip(chains, st):
                c = t if d == 0 else n_chunks - 1 - t
                rows = pl.ds(pl.multiple_of(c * CHUNK, CHUNK), CHUNK)
                both = _dot(jnp.concatenate([_embed(rm[d, p, rows, :]), _embed(mm[d, p, rows, :])], axis=0), s)
                y_s[d, p, rows, :] = _fold(both[:LANES]) + y0[d, p, rows, :]
                out.append(both[LANES:] + _embed(nn[d, p, rows, :]))
            return tuple(out)

        return lax.fori_loop(0, n_chunks, step, states)

    zero = jnp.zeros((LANES, LANES), F32)
    states = run(rm_c, y0_c, m_c, n_c, yc_s, tuple(zero for _ in chains))
    run(rm_l, y0_l, m_l, n_l, yl_s, states)

    ones = _head_ones()

    def finish(y_s, bv, g, o_ref):
        def tile(i, carry):
            rows = pl.ds(pl.multiple_of(i * RW_TILE, RW_TILE), RW_TILE)
            for p in range(SCAN_PAIRS):
                y = y_s[0, p, rows, :] + y_s[1, p, rows, :]
                dev = y - _dot_exact_rhs(y, ones, 2) * (1.0 / HEAD_DIM)
                var = _dot_exact_rhs(dev * dev, ones, 2) * (1.0 / HEAD_DIM)
                yn = dev * lax.rsqrt(var + GN_EPS) * vec_ref[p, 7:8, :] + vec_ref[p, 8:9, :]
                o_ref[rows, p * LANES:(p + 1) * LANES] = ((yn + bv[p, rows, :]) * g[p, rows, :]).astype(o_ref.dtype)
            return carry

        lax.fori_loop(0, y_s.shape[2] // RW_TILE, tile, 0)

    finish(yc_s, bv_c, g_c, oc_ref)
    finish(yl_s, bv_l, g_l, ol_ref)


def _rwkv_scan(rm, y0, mm, nn, bv, g, vecs, *, nb, lc, ls):
    pc = SCAN_PAIRS
    ctx_d = pl.BlockSpec((2, pc, lc, LANES), lambda b, j: (0, j, b, 0))
    lat0 = nb * lc // ls
    lat_d = pl.BlockSpec((2, pc, ls, LANES), lambda b, j: (0, j, lat0 + b, 0))
    ctx_1 = pl.BlockSpec((pc, lc, LANES), lambda b, j: (j, b, 0))
    lat_1 = pl.BlockSpec((pc, ls, LANES), lambda b, j: (j, lat0 + b, 0))
    return pl.pallas_call(
        _rwkv_scan_body,
        grid=(nb, N_PAIRS // pc),
        in_specs=[ctx_d, ctx_d, ctx_d, ctx_d, lat_d, lat_d, lat_d, lat_d, ctx_1, ctx_1, lat_1, lat_1,
                  pl.BlockSpec((pc, 16, LANES), lambda b, j: (j, 0, 0))],
        out_specs=[pl.BlockSpec((lc, pc * LANES), lambda b, j: (b, j)),
                   pl.BlockSpec((ls, pc * LANES), lambda b, j: (b, j))],
        out_shape=[jax.ShapeDtypeStruct((nb * lc, GROUP_W), BF16),
                   jax.ShapeDtypeStruct((nb * ls, GROUP_W), BF16)],
        scratch_shapes=[pltpu.VMEM((2, pc, lc, LANES), F32), pltpu.VMEM((2, pc, ls, LANES), F32)],
        compiler_params=_cparams(("parallel", "parallel")),
        name="rwkv7_recurrence_output",
    )(rm, y0, mm, nn, rm, y0, mm, nn, bv, g, bv, g, vecs)


def _gqa_attention(q_pairs, k, v, bias, sinks, stack):
    rows = q_pairs[0].shape[0]
    outs = []
    for first in range(0, 2 * N_PAIRS, stack):
        heads = range(first, first + stack)
        qs = jnp.concatenate([_to_half(q_pairs[h // 2], h % 2, h // N_PAIRS) for h in heads], axis=0)
        s = _dot_nt(qs, k)
        if bias is not None:
            s = s + jnp.tile(bias, (stack, 1))
        sink = None
        if sinks is not None:
            sink = jnp.concatenate([jnp.broadcast_to(sinks[h], (rows, 1)) for h in heads], axis=0)
        o = _softmax_pv(s, v, sink)
        outs += [o[i * rows:(i + 1) * rows] for i in range(stack)]
    return [_merge_heads(outs[2 * p], p // 2, outs[2 * p + 1], p // 2) for p in range(N_PAIRS)]


def _window_body(q_ref, kp_ref, kc_ref, kn_ref, vp_ref, vc_ref, vn_ref, kx_ref, vx_ref,
                 cq_ref, sq_ref, cp_ref, sp_ref, cn_ref, sn_ref, sink_ref, o_ref, *, n_blocks, lc):
    n = pl.program_id(1)
    q = q_ref[...]
    k_lat = jnp.concatenate([_rope(kp_ref[...], cp_ref[...], sp_ref[...]),
                             _rope(kc_ref[...], cq_ref[...], sq_ref[...]),
                             _rope(kn_ref[...], cn_ref[...], sn_ref[...])], axis=0)
    k = jnp.concatenate([kx_ref[...], k_lat], axis=0).astype(BF16)
    v = jnp.concatenate([vx_ref[...], vp_ref[...], vc_ref[...], vn_ref[...]], axis=0).astype(BF16)
    nk = lc + 3 * BLOCK
    qi = lax.broadcasted_iota(jnp.int32, (BLOCK, nk), 0)
    kj = lax.broadcasted_iota(jnp.int32, (BLOCK, nk), 1) - lc
    kblk = n - 1 + kj // BLOCK
    ok = (kj < 0) | ((jnp.abs(kj - BLOCK - qi) <= WINDOW) & (kblk >= 0) & (kblk < n_blocks))
    bias = jnp.where(ok, 0.0, NEG_INF)
    qp = [_rope(q[:, p * LANES:(p + 1) * LANES], cq_ref[...], sq_ref[...]) * Q_SCALE for p in range(N_PAIRS)]
    sinks = [sink_ref[h:h + 1, 0:1] for h in range(2 * N_PAIRS)]
    outs = _gqa_attention(qp, k, v, bias, sinks, stack=4)
    for p in range(N_PAIRS):
        o_ref[:, p * LANES:(p + 1) * LANES] = outs[p].astype(o_ref.dtype)


def _window_attention(qkv, cos, sin, sink, *, nb, lc, ls):
    n_blocks = ls // BLOCK
    base = nb * lc // BLOCK
    kcol, vcol = GROUP_W // LANES, GROUP_W // LANES + 1
    blk = lambda b, n, off: base + b * n_blocks + jnp.clip(n + off, 0, n_blocks - 1)
    kv_spec = lambda col, off: pl.BlockSpec((BLOCK, LANES), lambda b, n: (blk(b, n, off), col))
    rope_spec = lambda off: pl.BlockSpec((BLOCK, LANES), lambda b, n: (jnp.clip(n + off, 0, n_blocks - 1), 0))
    return pl.pallas_call(
        functools.partial(_window_body, n_blocks=n_blocks, lc=lc),
        grid=(nb, n_blocks),
        in_specs=[pl.BlockSpec((BLOCK, GROUP_W), lambda b, n: (blk(b, n, 0), 0)),
                  kv_spec(kcol, -1), kv_spec(kcol, 0), kv_spec(kcol, 1),
                  kv_spec(vcol, -1), kv_spec(vcol, 0), kv_spec(vcol, 1),
                  pl.BlockSpec((lc, LANES), lambda b, n: (b, kcol)),
                  pl.BlockSpec((lc, LANES), lambda b, n: (b, vcol)),
                  rope_spec(0), rope_spec(0), rope_spec(-1), rope_spec(-1), rope_spec(1), rope_spec(1),
                  pl.BlockSpec((8, LANES), lambda b, n: (0, 0))],
        out_specs=pl.BlockSpec((BLOCK, GROUP_W), lambda b, n: (b * n_blocks + n, 0)),
        out_shape=jax.ShapeDtypeStruct((nb * ls, GROUP_W), BF16),
        compiler_params=_cparams(("parallel", "parallel")),
        name="window_attention",
    )(qkv, qkv, qkv, qkv, qkv, qkv, qkv, qkv, qkv, cos, sin, cos, sin, cos, sin, sink)


def _global_body(q_ref, kx_ref, kl_ref, vx_ref, vl_ref, cq_ref, sq_ref, cos_ref, sin_ref, gq_ref, gk_ref,
                 o_ref, k_scr, v_scr, *, lc):
    @pl.when(pl.program_id(1) == 0)
    def _():
        k_scr[0:lc, :] = _head_rms(kx_ref[...], gk_ref[...]).astype(BF16)
        k_scr[lc:, :] = _rope(_head_rms(kl_ref[...], gk_ref[...]), cos_ref[...], sin_ref[...]).astype(BF16)
        v_scr[0:lc, :] = vx_ref[...].astype(BF16)
        v_scr[lc:, :] = vl_ref[...].astype(BF16)

    q = q_ref[...]
    k = k_scr[...]
    v = v_scr[...]
    qp = [_rope(_head_rms(q[:, p * LANES:(p + 1) * LANES], gq_ref[...]), cq_ref[...], sq_ref[...]) * Q_SCALE
          for p in range(N_PAIRS)]
    outs = _gqa_attention(qp, k, v, None, None, stack=2)
    for p in range(N_PAIRS):
        o_ref[:, p * LANES:(p + 1) * LANES] = outs[p].astype(o_ref.dtype)


def _global_attention(qkv, cos, sin, q_gain, k_gain, *, nb, lc, ls):
    n_blocks = ls // BLOCK
    base = nb * lc // BLOCK
    lat0 = nb * lc // ls
    kcol, vcol = GROUP_W // LANES, GROUP_W // LANES + 1
    return pl.pallas_call(
        functools.partial(_global_body, lc=lc),
        grid=(nb, n_blocks),
        in_specs=[pl.BlockSpec((BLOCK, GROUP_W), lambda b, n: (base + b * n_blocks + n, 0)),
                  pl.BlockSpec((lc, LANES), lambda b, n: (b, kcol)),
                  pl.BlockSpec((ls, LANES), lambda b, n: (lat0 + b, kcol)),
                  pl.BlockSpec((lc, LANES), lambda b, n: (b, vcol)),
                  pl.BlockSpec((ls, LANES), lambda b, n: (lat0 + b, vcol)),
                  pl.BlockSpec((BLOCK, LANES), lambda b, n: (n, 0)),
                  pl.BlockSpec((BLOCK, LANES), lambda b, n: (n, 0)),
                  pl.BlockSpec((ls, LANES), lambda b, n: (0, 0)),
                  pl.BlockSpec((ls, LANES), lambda b, n: (0, 0)),
                  pl.BlockSpec((1, LANES), lambda b, n: (0, 0)),
                  pl.BlockSpec((1, LANES), lambda b, n: (0, 0))],
        out_specs=pl.BlockSpec((BLOCK, GROUP_W), lambda b, n: (b * n_blocks + n, 0)),
        out_shape=jax.ShapeDtypeStruct((nb * ls, GROUP_W), BF16),
        scratch_shapes=[pltpu.VMEM((lc + ls, LANES), BF16), pltpu.VMEM((lc + ls, LANES), BF16)],
        compiler_params=_cparams(("parallel", "arbitrary")),
        name="global_attention",
    )(qkv, qkv, qkv, qkv, qkv, cos, sin, cos, sin, q_gain, k_gain)


def _na_body(q_ref, kx_ref, kl_ref, vx_ref, vl_ref, bias_ref, o_ref, *, n_rows, lc):
    r = pl.program_id(1)
    kh = min(NA_KH, n_rows)
    rs = jnp.clip(r - kh // 2, 0, n_rows - kh)
    win = pl.ds(pl.multiple_of(rs * GRID_W, GRID_W), kh * GRID_W)
    q = q_ref[...] * Q_SCALE
    for p in range(N_PAIRS):
        cols = slice(p * LANES, (p + 1) * LANES)
        k = jnp.concatenate([kx_ref[:, cols], kl_ref[win, cols]], axis=0).astype(BF16)
        v = jnp.concatenate([vx_ref[:, cols], vl_ref[win, cols]], axis=0).astype(BF16)
        s = _dot_nt(_embed(q[:, cols]), k)
        s = s + jnp.concatenate([jnp.zeros((2 * GRID_W, lc), F32), bias_ref[p]], axis=1)
        o = _softmax_pv(s, v)
        o_ref[:, cols] = jnp.where(_lane_half((GRID_W, LANES)), o[:GRID_W], o[GRID_W:]).astype(o_ref.dtype)


def _na_bias_table(rpb, n_rows):
    kh = min(NA_KH, n_rows)
    qc = np.arange(GRID_W)[:, None]
    kc = np.arange(GRID_W)[None, :]
    win_start = np.clip(qc - NA_KW // 2, 0, GRID_W - NA_KW)
    col_ok = (kc >= win_start) & (kc < win_start + NA_KW)
    dc = np.clip(kc - qc + NA_KW - 1, 0, 2 * NA_KW - 2)
    n_dc = 2 * NA_KW - 1
    heads, n_dr = rpb.shape[0], rpb.shape[1]
    onehot = (dc.reshape(1, -1) == np.arange(n_dc)[:, None]).astype(np.float32)
    by_col = jnp.dot(rpb.reshape(heads * n_dr, n_dc), onehot, precision=lax.Precision.HIGHEST)
    by_col = jnp.where(col_ok[None, None], by_col.reshape(heads, n_dr, GRID_W, GRID_W), NEG_INF)
    tab = jnp.stack([by_col[:, NA_KH - 1 - var:NA_KH - 1 - var + kh] for var in range(kh)])
    tab = tab.transpose(0, 1, 3, 2, 4).reshape(kh, N_PAIRS, 2 * GRID_W, kh * GRID_W)
    return (tab * LOG2E).astype(F32)


def _neighbourhood_attention(qkv, bias, *, nb, lc, ls):
    n_rows = ls // GRID_W
    kh = min(NA_KH, n_rows)
    base = nb * lc // GRID_W
    lat0 = nb * lc // ls

    def variant(b, r):
        return (r - jnp.clip(r - kh // 2, 0, n_rows - kh), 0, 0, 0)

    return pl.pallas_call(
        functools.partial(_na_body, n_rows=n_rows, lc=lc),
        grid=(nb, n_rows),
        in_specs=[pl.BlockSpec((GRID_W, GROUP_W), lambda b, r: (base + b * n_rows + r, 0)),
                  pl.BlockSpec((lc, GROUP_W), lambda b, r: (b, 1)),
                  pl.BlockSpec((ls, GROUP_W), lambda b, r: (lat0 + b, 1)),
                  pl.BlockSpec((lc, GROUP_W), lambda b, r: (b, 2)),
                  pl.BlockSpec((ls, GROUP_W), lambda b, r: (lat0 + b, 2)),
                  pl.BlockSpec((None, N_PAIRS, 2 * GRID_W, kh * GRID_W), variant)],
        out_specs=pl.BlockSpec((GRID_W, GROUP_W), lambda b, r: (b * n_rows + r, 0)),
        out_shape=jax.ShapeDtypeStruct((nb * ls, GROUP_W), BF16),
        compiler_params=_cparams(("parallel", "parallel")),
        name="neighbourhood_attention",
    )(qkv, qkv, qkv, qkv, qkv, bias)


def _ctx_body(q_ref, k_ref, v_ref, sink_ref, gq_ref, gk_ref, o_ref, *, gqa, use_sink, use_norm):
    q = q_ref[...]
    k = k_ref[...]
    v = v_ref[...].astype(BF16)
    if gqa:
        if use_norm:
            k = _head_rms(k, gk_ref[...])
        k = k.astype(BF16)
        qp = []
        for p in range(N_PAIRS):
            x = q[:, p * LANES:(p + 1) * LANES]
            if use_norm:
                x = _head_rms(x, gq_ref[...])
            qp.append(x * Q_SCALE)
        sinks = [sink_ref[h:h + 1, 0:1] for h in range(2 * N_PAIRS)] if use_sink else None
        outs = _gqa_attention(qp, k, v, None, sinks, stack=4)
        for p in range(N_PAIRS):
            o_ref[:, p * LANES:(p + 1) * LANES] = outs[p].astype(o_ref.dtype)
    else:
        rows = q.shape[0]
        for p in range(N_PAIRS):
            cols = slice(p * LANES, (p + 1) * LANES)
            o = _softmax_pv(_dot_nt(_embed(q[:, cols] * Q_SCALE), k[:, cols]), v[:, cols])
            o_ref[:, cols] = jnp.where(_lane_half((rows, LANES)), o[:rows], o[rows:]).astype(o_ref.dtype)


def _ctx_attention(qkv, *, nb, lc, gqa, q_col=0, sink=None, q_gain=None, k_gain=None):
    kw = KV_W if gqa else GROUP_W
    kcol = (q_col * GROUP_W + GROUP_W) // kw
    dummy = jnp.zeros((8, LANES), F32)
    one = jnp.ones((1, LANES), F32)
    return pl.pallas_call(
        functools.partial(_ctx_body, gqa=gqa, use_sink=sink is not None, use_norm=q_gain is not None),
        grid=(nb,),
        in_specs=[pl.BlockSpec((lc, GROUP_W), lambda b: (b, q_col)),
                  pl.BlockSpec((lc, kw), lambda b: (b, kcol)),
                  pl.BlockSpec((lc, kw), lambda b: (b, kcol + 1)),
                  pl.BlockSpec((8, LANES), lambda b: (0, 0)),
                  pl.BlockSpec((1, LANES), lambda b: (0, 0)),
                  pl.BlockSpec((1, LANES), lambda b: (0, 0))],
        out_specs=pl.BlockSpec((lc, GROUP_W), lambda b: (b, 0)),
        out_shape=jax.ShapeDtypeStruct((nb * lc, GROUP_W), BF16),
        compiler_params=_cparams(("parallel",)),
        name="context_attention",
    )(qkv, qkv, qkv, dummy if sink is None else sink, one if q_gain is None else q_gain,
      one if k_gain is None else k_gain)


def _rope_tables(ls):
    t = jnp.arange(ls, dtype=jnp.int32)
    n_freq = HEAD_DIM // 4
    inv_freq = ROPE_BASE ** (-jnp.arange(n_freq, dtype=F32) / n_freq)
    row = (t // GRID_W).astype(F32)[:, None] * inv_freq
    col = (t % GRID_W).astype(F32)[:, None] * inv_freq
    cos = jnp.concatenate([jnp.cos(row), jnp.cos(row), jnp.cos(col), jnp.cos(col)], axis=-1)
    sin = jnp.concatenate([-jnp.sin(row), jnp.sin(row), -jnp.sin(col), jnp.sin(col)], axis=-1)
    return jnp.tile(cos, (1, 2)), jnp.tile(sin, (1, 2))


def _rwkv_params(mu_prev, mu_next, w0, w2, a0, a2, g2, k_k, k_a, r_k, gn_w, gn_b):
    aw = GROUP_W
    pair = lambda x: x.reshape(N_PAIRS, LANES)

    def mu_rkv(mu):
        return jnp.concatenate([pair(mu[0:aw]), pair(mu[aw:2 * aw]), pair(mu[2 * aw:3 * aw])], axis=-1)

    mu_a = jnp.zeros((N_PAIRS, 8, 3 * LANES), F32)
    mu_a = mu_a.at[:, 0].set(mu_rkv(mu_prev)).at[:, 1].set(mu_rkv(mu_next))
    mu_l = jnp.zeros((8, 3 * LANES), F32).at[0].set(mu_prev[3 * aw:]).at[1].set(mu_next[3 * aw:])
    vecs = jnp.zeros((N_PAIRS, 16, LANES), F32)
    rows = [pair(w0[0]), pair(w0[1]), pair(a0[0]), pair(a0[1]), pair(k_k), pair(k_a), pair(r_k.reshape(-1)),
            pair(gn_w), pair(gn_b)]
    for i, val in enumerate(rows):
        vecs = vecs.at[:, i].set(val)

    def lora_pad(w):
        out = jnp.zeros((2, N_PAIRS, LANES, LANES), F32)
        for d in range(2):
            blk = w[d].reshape(DECAY_LORA, N_PAIRS, LANES).transpose(1, 0, 2)
            out = out.at[d, :, d * DECAY_LORA:(d + 1) * DECAY_LORA, :].set(blk)
        return out

    g2p = g2.reshape(GATE_LORA, N_PAIRS, LANES).transpose(1, 0, 2)
    return dict(mu_rkv=mu_a, mu_lo=mu_l, vecs=vecs, w2=lora_pad(w2), a2=lora_pad(a2), g2=g2p)


def _lane_bcast(x, rows):
    return jnp.zeros((rows, LANES), F32).at[:x.shape[0]].set(jnp.broadcast_to(x[:, None], (x.shape[0], LANES)))


A_W = GROUP_W
A_IN = 3 * A_W + 3 * LANES
B_IN = GROUP_W + 2 * KV_W


def _even_mixer(h, mods, gains, w_in, layer, w_layer, a_prm, sink, cos, sin, *, dims):
    nb, lc, ls = dims["nb"], dims["lc"], dims["ls"]
    t = nb * (lc + ls)
    rkv_dests = [(0, j, part * LANES, part * A_W + j * LANES, LANES)
                 for j in range(N_PAIRS) for part in range(3)]
    pieces = [(0, 3 * A_W, rkv_dests),
              (3 * A_W, 3 * LANES, [(1, None, 0, 0, 3 * LANES)]),
              (A_IN, B_IN, [(2, None, 0, 0, B_IN)])]
    rkv, lo, qkv = _in_proj(h, mods, gains, w_in, layer, w_layer,
                            [(N_PAIRS, t, 3 * LANES), (t, 3 * LANES), (t, B_IN)], pieces,
                            tm=dims["tm"], n_ctx_rows=nb * lc, ls=ls, nb=nb)
    rm, y0, mm, nn, bv, g = _rwkv_precompute(rkv, lo, a_prm, n_ctx_rows=nb * lc, lc=lc, ls=ls)
    a_ctx, a_lat = _rwkv_scan(rm, y0, mm, nn, bv, g, a_prm["vecs"], nb=nb, lc=lc, ls=ls)
    sink_b = _lane_bcast(sink * LOG2E, 8)
    b_lat = _window_attention(qkv, cos, sin, sink_b, nb=nb, lc=lc, ls=ls)
    b_ctx = _ctx_attention(qkv, nb=nb, lc=lc, gqa=True, sink=sink_b)
    return (a_ctx, a_lat), (b_ctx, b_lat)


def _odd_mixer(h, mods, gains, w_in, layer, w_layer, rpb, q_gain, k_gain, cos, sin, *, dims):
    nb, lc, ls = dims["nb"], dims["lc"], dims["ls"]
    t = nb * (lc + ls)
    c_in = 3 * GROUP_W
    pieces = [(0, c_in, [(0, None, 0, 0, c_in)]), (c_in, B_IN, [(1, None, 0, 0, B_IN)])]
    qkv_c, qkv_d = _in_proj(h, mods, gains, w_in, layer, w_layer, [(t, c_in), (t, B_IN)], pieces,
                            tm=dims["tm"], n_ctx_rows=nb * lc, ls=ls, nb=nb)
    bias = _na_bias_table(rpb, ls // GRID_W)
    c_lat = _neighbourhood_attention(qkv_c, bias, nb=nb, lc=lc, ls=ls)
    c_ctx = _ctx_attention(qkv_c, nb=nb, lc=lc, gqa=False)
    gq = jnp.tile(q_gain, 2)[None, :]
    gk = jnp.tile(k_gain, 2)[None, :]
    d_lat = _global_attention(qkv_d, cos, sin, gq, gk, nb=nb, lc=lc, ls=ls)
    d_ctx = _ctx_attention(qkv_d, nb=nb, lc=lc, gqa=True, q_gain=gq, k_gain=gk)
    return (c_ctx, c_lat), (d_ctx, d_lat)


def _row_tile(target, rows_ctx, ls):
    tile = target
    while rows_ctx % tile or ls % tile:
        tile //= 2
    return tile


def kernel(x, c, ctx, c_ctx, w_ada, b_ada, g_pre_mix, g_post_mix, g_pre_ff, g_post_ff, w_in_even, w_in_odd, w_out, w_ff1, w_ff2, a_mu_prev, a_mu_next, a_w0, a_w2, a_a0, a_a2, a_g2, a_k_k, a_k_a, a_r_k, a_gn_w, a_gn_b, b_sink, c_rpb, d_q_gain, d_k_gain):
    nb, ls, d = x.shape
    lc = ctx.shape[1]
    depth = w_ada.shape[0]
    assert nb < 16 and lc % RW_TILE == 0 and ls % RW_TILE == 0 and (nb * lc) % ls == 0
    n_ctx_rows = nb * lc
    tm = _row_tile(512, n_ctx_rows, ls)
    dims = dict(nb=nb, lc=lc, ls=ls, tm=tm)
    cvec = jnp.zeros((16, d), F32).at[:nb].set(c).at[nb].set(c_ctx)
    mods = _modulation(cvec, w_ada, b_ada)
    cos, sin = _rope_tables(ls)
    stack = lambda g: g.reshape(depth, 1, d)
    g_pre_mix, g_post_mix, g_pre_ff, g_post_ff = map(stack, (g_pre_mix, g_post_mix, g_pre_ff, g_post_ff))
    w_in_even, w_in_odd, w_out, w_ff1, w_ff2 = (w.astype(BF16) for w in (w_in_even, w_in_odd, w_out, w_ff1, w_ff2))
    h = (ctx.reshape(n_ctx_rows, d), x.reshape(nb * ls, d))
    for i in range(depth):
        j = i // 2
        last = i == depth - 1
        if i % 2 == 0:
            a_prm = _rwkv_params(a_mu_prev[j], a_mu_next[j], a_w0[j], a_w2[j], a_a0[j], a_a2[j], a_g2[j],
                                 a_k_k[j], a_k_a[j], a_r_k[j], a_gn_w[j], a_gn_b[j])
            mix_a, mix_b = _even_mixer(h, mods, g_pre_mix, w_in_even, i, j, a_prm, b_sink[j], cos, sin, dims=dims)
        else:
            mix_a, mix_b = _odd_mixer(h, mods, g_pre_mix, w_in_odd, i, j, c_rpb[j], d_q_gain[j], d_k_gain[j],
                                      cos, sin, dims=dims)
        h = _out_proj(mix_a, mix_b, h, mods, g_post_mix, w_out, i, tm=tm, n_ctx_rows=n_ctx_rows, ls=ls, nb=nb)
        h = _mlp(h, mods, g_pre_ff, g_post_ff, w_ff1, w_ff2, i, tm=tm, n_split=2, n_ctx_rows=n_ctx_rows, ls=ls,
                 nb=nb, first_row=n_ctx_rows if last else 0)
        h = (h,)
    return h[0].reshape(nb, ls, d)
```

```python
import functools

import numpy as np
import jax
import jax.numpy as jnp
from jax import lax
from jax.experimental import pallas as pl
from jax.experimental.pallas import tpu as pltpu

F32 = jnp.float32
BF16 = jnp.bfloat16

HEAD_DIM = 64
LANES = 128
N_PAIRS = 4
GROUP_W = N_PAIRS * LANES
KV_W = LANES
DECAY_LORA = 64
GATE_LORA = 128
WINDOW = 128
BLOCK = 128
GRID_W = 64
NA_KH = 8
NA_KW = 16
ROPE_BASE = 10000.0
NORM_EPS = 1e-6
GN_EPS = 64e-5
NEG_INF = -1e30
ATTN_SCALE = HEAD_DIM ** -0.5
LOG2E = 1.4426950408889634
Q_SCALE = ATTN_SCALE * LOG2E
CHUNK = 64
RW_TILE = 256
VMEM_LIMIT = 56 * 1024 * 1024


def _cparams(sem):
    return pltpu.CompilerParams(dimension_semantics=sem, vmem_limit_bytes=VMEM_LIMIT)


def _dot(a, b):
    return jnp.dot(a.astype(BF16), b.astype(BF16), preferred_element_type=F32)


def _dot_nt(a, b):
    return lax.dot_general(a.astype(BF16), b.astype(BF16), (((1,), (1,)), ((), ())),
                           preferred_element_type=F32)


def _dot_tn(a, b):
    return lax.dot_general(a.astype(BF16), b.astype(BF16), (((0,), (0,)), ((), ())),
                           preferred_element_type=F32)


def _dot_f32(a, b):
    return jnp.dot(a, b, preferred_element_type=F32, precision=lax.Precision.HIGHEST)


def _dot_exact_rhs(x, b, pieces):
    b = b.astype(BF16)
    acc = None
    for _ in range(pieces):
        part = x.astype(BF16)
        term = jnp.dot(part, b, preferred_element_type=F32)
        acc = term if acc is None else acc + term
        x = x - part.astype(F32)
    return acc


def _dot_exact_lhs(a, x, pieces):
    a = a.astype(BF16)
    acc = None
    for _ in range(pieces):
        part = x.astype(BF16)
        term = jnp.dot(a, part, preferred_element_type=F32)
        acc = term if acc is None else acc + term
        x = x - part.astype(F32)
    return acc


def _lane_half(shape):
    return lax.broadcasted_iota(jnp.int32, shape, len(shape) - 1) % LANES < HEAD_DIM


def _head_ones():
    r = lax.broadcasted_iota(jnp.int32, (LANES, LANES), 0) // HEAD_DIM
    c = lax.broadcasted_iota(jnp.int32, (LANES, LANES), 1) // HEAD_DIM
    return (r == c).astype(F32)


def _embed(x):
    first = _lane_half(x.shape)
    zero = jnp.zeros_like(x)
    return jnp.concatenate([jnp.where(first, x, zero), jnp.where(first, zero, x)], axis=0)


def _fold(x):
    n = x.shape[0] // 2
    return x[:n] + x[n:]


def _sigmoid(x):
    return 1.0 / (1.0 + jnp.exp(-x))


def _rms_rows(x, gain):
    return x * lax.rsqrt(jnp.mean(x * x, axis=-1, keepdims=True) + NORM_EPS) * gain


def _rope(x, cos, sin):
    lane = lax.broadcasted_iota(jnp.int32, x.shape, 1)
    swapped = jnp.where(lane % 32 < 16, pltpu.roll(x, LANES - 16, 1), pltpu.roll(x, 16, 1))
    return x * cos + swapped * sin


def _head_rms(x, gain):
    ms = _dot_exact_rhs(x * x, _head_ones(), 2) * (1.0 / HEAD_DIM)
    return x * lax.rsqrt(ms + NORM_EPS) * gain


def _to_half(x, src, dst):
    if src != dst:
        x = pltpu.roll(x, HEAD_DIM, 1)
    first = _lane_half(x.shape)
    keep = first if dst == 0 else jnp.logical_not(first)
    return jnp.where(keep, x, jnp.zeros_like(x))


def _softmax_pv(s, v, sink=None):
    m = jnp.max(s, axis=-1, keepdims=True)
    if sink is not None:
        m = jnp.maximum(m, sink)
    p = jnp.exp2(s - m)
    den = jnp.sum(p, axis=-1, keepdims=True)
    if sink is not None:
        den = den + jnp.exp2(sink - m)
    return _dot(p, v) / den


def _merge_heads(o0, src0, o1, src1):
    a = o0 if src0 == 0 else pltpu.roll(o0, HEAD_DIM, 1)
    b = o1 if src1 == 1 else pltpu.roll(o1, HEAD_DIM, 1)
    return jnp.where(_lane_half(a.shape), a, b)


def _mod_body(c_ref, w_ref, b_ref, o_ref):
    c = c_ref[...]
    s = c * _sigmoid(c)
    o_ref[...] = _dot_f32(s, w_ref[...]) + b_ref[...]


def _modulation(cvec, w_ada, b_ada):
    depth, d, _ = w_ada.shape
    out = pl.pallas_call(
        _mod_body,
        grid=(depth, 6),
        in_specs=[pl.BlockSpec((16, d), lambda l, n: (0, 0)),
                  pl.BlockSpec((None, d, d), lambda l, n: (l, 0, n)),
                  pl.BlockSpec((None, None, 1, d), lambda l, n: (l, n, 0, 0))],
        out_specs=pl.BlockSpec((None, None, 16, d), lambda l, n: (l, n, 0, 0)),
        out_shape=jax.ShapeDtypeStruct((depth, 6, 16, d), F32),
        compiler_params=_cparams(("parallel", "parallel")),
        name="adaln_modulation",
    )(cvec, w_ada, b_ada.reshape(depth, 6, 1, d))
    return out.transpose(0, 2, 1, 3)


def _mod_row(tile, tm, n_ctx_rows, ls, nb):
    start = tile * tm
    return jnp.where(start < n_ctx_rows, nb, (start - n_ctx_rows) // ls)


def _row_specs(arrs, tm, n_ctx_tiles, off=0):
    width = arrs[0].shape[1]
    if len(arrs) == 1:
        return [pl.BlockSpec((tm, width), lambda i: (i + off, 0))]
    return [pl.BlockSpec((tm, width), lambda i: (jnp.minimum(i + off, n_ctx_tiles - 1), 0)),
            pl.BlockSpec((tm, width), lambda i: (jnp.maximum(i + off - n_ctx_tiles, 0), 0))]


def _read_rows(refs, n_ctx_tiles, off=0):
    if len(refs) == 1:
        return refs[0][...]
    return jnp.where(pl.program_id(0) + off < n_ctx_tiles, refs[0][...], refs[1][...])


def _inproj_body(*refs, pieces, n_h, n_ctx_tiles):
    x = _read_rows(refs[:n_h], n_ctx_tiles)
    m_ref, g_ref, w_ref = refs[n_h:n_h + 3]
    o_refs = refs[n_h + 3:]
    u = _rms_rows(x, g_ref[...]) * (1.0 + m_ref[1:2, :]) + m_ref[0:1, :]
    u = u.astype(BF16)
    for (src, span, dests) in pieces:
        val = jnp.dot(u, w_ref[:, src:src + span], preferred_element_type=F32)
        for (oi, lead, dst, off, width) in dests:
            if lead is None:
                o_refs[oi][:, dst:dst + width] = val[:, off:off + width]
            else:
                o_refs[oi][lead, :, dst:dst + width] = val[:, off:off + width]


def _in_proj(h, mods, gains, w, layer, w_layer, out_defs, pieces, *, tm, n_ctx_rows, ls, nb):
    t = sum(a.shape[0] for a in h)
    d = h[0].shape[1]
    n_in = w.shape[2]
    nct = n_ctx_rows // tm
    out_shapes, out_specs = [], []
    for shape in out_defs:
        out_shapes.append(jax.ShapeDtypeStruct(shape, F32))
        if len(shape) == 3:
            out_specs.append(pl.BlockSpec((shape[0], tm, shape[2]), lambda i: (0, i, 0)))
        else:
            out_specs.append(pl.BlockSpec((tm, shape[1]), lambda i: (i, 0)))
    row = functools.partial(_mod_row, tm=tm, n_ctx_rows=n_ctx_rows, ls=ls, nb=nb)
    return pl.pallas_call(
        functools.partial(_inproj_body, pieces=pieces, n_h=len(h), n_ctx_tiles=nct),
        grid=(t // tm,),
        in_specs=_row_specs(h, tm, nct) + [
            pl.BlockSpec((None, None, 6, d), lambda i: (layer, row(i), 0, 0)),
            pl.BlockSpec((None, 1, d), lambda i: (layer, 0, 0)),
            pl.BlockSpec((None, d, n_in), lambda i: (w_layer, 0, 0))],
        out_specs=out_specs,
        out_shape=out_shapes,
        compiler_params=_cparams(("parallel",)),
        name="modulate_in_proj",
    )(*h, mods, gains, w)


def _mix_mlp_body(*refs, n_h, n_ctx_tiles, off, n_split):
    a = _read_rows(refs[0:2], n_ctx_tiles, off)
    b = _read_rows(refs[2:4], n_ctx_tiles, off)
    x = _read_rows(refs[4:4 + n_h], n_ctx_tiles, off)
    m_ref, gm_ref, g1_ref, g2_ref, wo_ref, w1_ref, w2_ref, o_ref = refs[4 + n_h:]
    o = (jnp.dot(a, wo_ref[0:GROUP_W, :], preferred_element_type=F32)
         + jnp.dot(b, wo_ref[GROUP_W:2 * GROUP_W, :], preferred_element_type=F32))
    x = x + m_ref[2:3, :] * _rms_rows(o, gm_ref[...])
    u = (_rms_rows(x, g1_ref[...]) * (1.0 + m_ref[4:5, :]) + m_ref[3:4, :]).astype(BF16)
    fc = w1_ref.shape[1] // n_split
    acc = None
    for c in range(n_split):
        f = jnp.dot(u, w1_ref[:, c * fc:(c + 1) * fc], preferred_element_type=F32)
        f = jnp.square(jnp.maximum(f, 0.0)).astype(BF16)
        part = jnp.dot(f, w2_ref[c * fc:(c + 1) * fc, :], preferred_element_type=F32)
        acc = part if acc is None else acc + part
    o_ref[...] = x + m_ref[5:6, :] * _rms_rows(acc, g2_ref[...])


def _mix_mlp(mix_a, mix_b, h, mods, g_post_mix, g_pre_ff, g_post_ff, w_out, w1, w2, layer, *,
             tm, n_split, n_ctx_rows, ls, nb, first_row):
    t = sum(a.shape[0] for a in h)
    d = h[0].shape[1]
    dff = w1.shape[2]
    nct = n_ctx_rows // tm
    off = first_row // tm
    row = functools.partial(_mod_row, tm=tm, n_ctx_rows=n_ctx_rows, ls=ls, nb=nb)
    once = pl.Buffered(1)
    gain = pl.BlockSpec((None, 1, d), lambda i: (layer, 0, 0))
    return pl.pallas_call(
        functools.partial(_mix_mlp_body, n_h=len(h), n_ctx_tiles=nct, off=off, n_split=n_split),
        grid=((t - first_row) // tm,),
        in_specs=(_row_specs(mix_a, tm, nct, off) + _row_specs(mix_b, tm, nct, off) + _row_specs(h, tm, nct, off) + [
            pl.BlockSpec((None, None, 6, d), lambda i: (layer, row(i + off), 0, 0)), gain, gain, gain,
            pl.BlockSpec((None, 2 * GROUP_W, d), lambda i: (layer, 0, 0), pipeline_mode=once),
            pl.BlockSpec((None, d, dff), lambda i: (layer, 0, 0), pipeline_mode=once),
            pl.BlockSpec((None, dff, d), lambda i: (layer, 0, 0), pipeline_mode=once)]),
        out_specs=pl.BlockSpec((tm, d), lambda i: (i, 0)),
        out_shape=jax.ShapeDtypeStruct((t - first_row, d), F32),
        compiler_params=_cparams(("parallel",)),
        name="out_proj_mlp_residuals",
    )(*mix_a, *mix_b, *h, mods, g_post_mix, g_pre_ff, g_post_ff, w_out, w1, w2)


def _rwkv_units(units):
    n = units[0][0].shape[0]
    n2 = 2 * n
    rev = [u[6] for u in units]
    ri = lax.broadcasted_iota(jnp.int32, (n, n), 0)
    ci = lax.broadcasted_iota(jnp.int32, (n, n), 1)
    tri = {False: (ci <= ri).astype(F32), True: (ci >= ri).astype(F32)}
    rr = lax.broadcasted_iota(jnp.int32, (n2, n2), 0)
    cc = lax.broadcasted_iota(jnp.int32, (n2, n2), 1)
    r2, c2 = rr % n, cc % n
    incl2 = {False: c2 <= r2, True: c2 >= r2}
    strict2 = {False: c2 < r2, True: c2 > r2}
    eye = rr == cc
    eye_f = jnp.where(eye, 1.0, 0.0)
    zero = jnp.zeros((n2, n2), F32)
    base = 8
    diag_blk = rr // base == cc // base
    sizes = []
    size = base
    while size < n:
        sizes.append(size)
        size *= 2
    off_blk = [(rr // (2 * s) == cc // (2 * s)) & (rr // s != cc // s) for s in sizes]

    cs = [_dot_exact_lhs(tri[u[6]], u[5], 3) for u in units]
    ops = []
    for (r, v, kd, alpha, beta, logw, _), c in zip(units, cs):
        c_all = jnp.sum(logw, axis=0, keepdims=True)
        e_pos = jnp.exp(c)
        e_neg = jnp.exp(-c)
        e_end = jnp.exp(c_all - c)
        ops.append(dict(rt=_embed(r * e_pos), at=_embed(alpha * jnp.exp(c - logw)), kt=_embed(kd * e_neg),
                        bt=_embed(beta * e_neg), kh=_embed(kd * e_end), bh=_embed(beta * e_end), vb=_embed(v),
                        decay=jnp.exp(c_all)))
    a_all = [_dot_nt(jnp.concatenate([o["rt"], o["at"]], axis=0), jnp.concatenate([o["kt"], o["bt"]], axis=0))
             for o in ops]
    a_qk = [jnp.where(incl2[f], a[:n2, :n2], zero) for a, f in zip(a_all, rev)]
    a_qb = [jnp.where(incl2[f], a[:n2, n2:], zero) for a, f in zip(a_all, rev)]
    a_ak = [jnp.where(strict2[f], a[n2:, :n2], zero) for a, f in zip(a_all, rev)]
    a_ab = [jnp.where(strict2[f], a[n2:, n2:], zero) for a, f in zip(a_all, rev)]
    akv = [_dot(a, o["vb"]) for a, o in zip(a_ak, ops)]
    apow = [jnp.where(diag_blk, a, zero) for a in a_ab]
    inv = [eye_f + a for a in apow]
    d2 = [_dot(a, a) for a in apow]
    both = [_dot(jnp.concatenate([a, x], axis=0), a) for a, x in zip(d2, inv)]
    inv = [x + b[n2:] for x, b in zip(inv, both)]
    inv = [x + _dot(x, b[:n2]) for x, b in zip(inv, both)]
    for blk in off_blk:
        right = [_dot(jnp.where(blk, a, zero), x) for a, x in zip(a_ab, inv)]
        inv = [x + _dot(x, y) for x, y in zip(inv, right)]
    wu = [_dot(x, jnp.concatenate([o["at"], y], axis=1)) for x, o, y in zip(inv, ops, akv)]
    outs = []
    for o, x, qb, qk in zip(ops, wu, a_qb, a_qk):
        w = x[:, :LANES]
        u0v = jnp.concatenate([x[:, LANES:], o["vb"]], axis=0)
        m = jnp.where(eye, o["decay"], 0.0) + _dot_tn(o["bh"], w)
        nn = _dot_tn(jnp.concatenate([o["bh"], o["kh"]], axis=0), u0v)
        rm = o["rt"] + _dot(qb, w)
        y0 = _dot(jnp.concatenate([qb, qk], axis=1), u0v)
        outs.append((_fold(rm), _fold(y0), _fold(m), _fold(nn)))
    return outs


def _shifted(x, prev_row, next_row, mu_prev, mu_next):
    rows = x.shape[0]
    ridx = lax.broadcasted_iota(jnp.int32, x.shape, 0)
    prev = jnp.where(ridx == 0, prev_row, pltpu.roll(x, 1, 0))
    nxt = jnp.where(ridx == rows - 1, next_row, pltpu.roll(x, rows - 1, 0))
    return x + mu_prev * (prev - x) + mu_next * (nxt - x)


def _rwkv_pre_body(rkv_ref, rkv_p_ref, rkv_n_ref, lo_ref, lo_p_ref, lo_n_ref,
                   mu_rkv_ref, mu_lo_ref, vec_ref, w2_ref, a2_ref, g2_ref,
                   rm_ref, y0_ref, m_ref, n_ref, bv_ref, g_ref,
                   r_s, v_s, kd_s, al_s, be_s, lw_s, lora_s, *, lat_tiles, ctx_tiles, n_ctx_tiles):
    i = pl.program_id(0)
    lat = i >= n_ctx_tiles
    pos = jnp.where(lat, (i - n_ctx_tiles) % lat_tiles, i % ctx_tiles)
    has_prev = (pos != 0).astype(F32)
    has_next = (pos != jnp.where(lat, lat_tiles, ctx_tiles) - 1).astype(F32)
    rkv = _shifted(rkv_ref[...], rkv_p_ref[7:8, :] * has_prev, rkv_n_ref[0:1, :] * has_next,
                   mu_rkv_ref[0:1, :], mu_rkv_ref[1:2, :])

    @pl.when(pl.program_id(1) == 0)
    def _():
        lo = _shifted(lo_ref[...], lo_p_ref[7:8, :] * has_prev, lo_n_ref[0:1, :] * has_next,
                      mu_lo_ref[0:1, :], mu_lo_ref[1:2, :])
        lora_s[0] = jnp.tanh(lo[:, 0:LANES])
        lora_s[1] = lo[:, LANES:2 * LANES]
        lora_s[2] = _sigmoid(lo[:, 2 * LANES:3 * LANES])

    r = rkv[:, 0:LANES]
    k = rkv[:, LANES:2 * LANES]
    v = rkv[:, 2 * LANES:3 * LANES]
    wl = lora_s[0]
    al = lora_s[1]
    gl = lora_s[2]
    ones = _head_ones()
    kx = k * vec_ref[4:5, :]
    kk = kx / jnp.maximum(jnp.sqrt(_dot_exact_rhs(kx * kx, ones, 2)), 1e-12)
    r_s[...] = r
    v_s[...] = v
    al_s[...] = -kk
    ksum = jnp.zeros_like(k)
    for d in range(2):
        w_raw = vec_ref[d:d + 1, :] + _dot(wl, w2_ref[d])
        z = -w_raw
        softplus = jnp.maximum(z, 0.0) + jnp.log(1.0 + jnp.exp(-jnp.abs(z)))
        lw_s[d] = -jnp.exp(-softplus - 0.5)
        iclr = _sigmoid(vec_ref[2 + d:3 + d, :] + _dot(al, a2_ref[d]))
        kd = k * (1.0 + (iclr - 1.0) * vec_ref[5:6, :])
        kd_s[d] = kd
        be_s[d] = kk * iclr
        ksum = ksum + kd
    bonus = _dot_exact_rhs(r * ksum * vec_ref[6:7, :], ones, 2)
    bv_ref[...] = bonus * v
    g_ref[...] = _dot(gl, g2_ref[...])

    units = []
    for ci in range(RW_TILE // CHUNK):
        rows = slice(ci * CHUNK, (ci + 1) * CHUNK)
        for d in range(2):
            units.append((r_s[rows, :], v_s[rows, :], kd_s[d, rows, :], al_s[rows, :], be_s[d, rows, :],
                          lw_s[d, rows, :], d == 1))
    outs = _rwkv_units(units)
    for idx, (rm, y0, m, nn) in enumerate(outs):
        ci, d = divmod(idx, 2)
        rows = slice(ci * CHUNK, (ci + 1) * CHUNK)
        rm_ref[d, rows, :] = rm.astype(BF16)
        y0_ref[d, rows, :] = y0
        m_ref[d, rows, :] = m.astype(BF16)
        n_ref[d, rows, :] = nn


def _rwkv_precompute(rkv, lo, prm, *, n_ctx_rows, lc, ls):
    _, t, _ = rkv.shape
    n_tiles = t // RW_TILE
    halo = RW_TILE // 8
    last_halo = t // 8 - 1
    tile = lambda i, j: (j, i, 0)
    prev = lambda i, j: (j, jnp.maximum(i * halo - 1, 0), 0)
    nxt = lambda i, j: (j, jnp.minimum((i + 1) * halo, last_halo), 0)
    out_dir = jax.ShapeDtypeStruct((2, N_PAIRS, t, LANES), F32)
    out_mat = jax.ShapeDtypeStruct((2, N_PAIRS, t, LANES), BF16)
    out_one = jax.ShapeDtypeStruct((N_PAIRS, t, LANES), F32)
    dir_spec = pl.BlockSpec((2, None, RW_TILE, LANES), lambda i, j: (0, j, i, 0))
    one_spec = pl.BlockSpec((None, RW_TILE, LANES), tile)
    body = functools.partial(_rwkv_pre_body, lat_tiles=ls // RW_TILE, ctx_tiles=lc // RW_TILE,
                             n_ctx_tiles=n_ctx_rows // RW_TILE)
    return pl.pallas_call(
        body,
        grid=(n_tiles, N_PAIRS),
        in_specs=[pl.BlockSpec((None, RW_TILE, 3 * LANES), tile),
                  pl.BlockSpec((None, 8, 3 * LANES), prev),
                  pl.BlockSpec((None, 8, 3 * LANES), nxt),
                  pl.BlockSpec((RW_TILE, 3 * LANES), lambda i, j: (i, 0)),
                  pl.BlockSpec((8, 3 * LANES), lambda i, j: (jnp.maximum(i * halo - 1, 0), 0)),
                  pl.BlockSpec((8, 3 * LANES), lambda i, j: (jnp.minimum((i + 1) * halo, last_halo), 0)),
                  pl.BlockSpec((None, 8, 3 * LANES), lambda i, j: (j, 0, 0)),
                  pl.BlockSpec((8, 3 * LANES), lambda i, j: (0, 0)),
                  pl.BlockSpec((None, 16, LANES), lambda i, j: (j, 0, 0)),
                  pl.BlockSpec((2, None, LANES, LANES), lambda i, j: (0, j, 0, 0)),
                  pl.BlockSpec((2, None, LANES, LANES), lambda i, j: (0, j, 0, 0)),
                  pl.BlockSpec((None, LANES, LANES), lambda i, j: (j, 0, 0))],
        out_specs=[dir_spec, dir_spec, dir_spec, dir_spec, one_spec, one_spec],
        out_shape=[out_mat, out_dir, out_mat, out_dir, out_one, out_one],
        scratch_shapes=[pltpu.VMEM((RW_TILE, LANES), F32), pltpu.VMEM((RW_TILE, LANES), F32),
                        pltpu.VMEM((2, RW_TILE, LANES), F32), pltpu.VMEM((RW_TILE, LANES), F32),
                        pltpu.VMEM((2, RW_TILE, LANES), F32), pltpu.VMEM((2, RW_TILE, LANES), F32),
                        pltpu.VMEM((3, RW_TILE, LANES), F32)],
        compiler_params=_cparams(("parallel", "arbitrary")),
        name="rwkv7_chunk_precompute",
    )(rkv, rkv, rkv, lo, lo, lo, prm["mu_rkv"], prm["mu_lo"], prm["vecs"], prm["w2"], prm["a2"], prm["g2"])


SCAN_PAIRS = 2


def _rwkv_scan_body(rm_c, y0_c, m_c, n_c, rm_l, y0_l, m_l, n_l, bv_c, g_c, bv_l, g_l, vec_ref,
                    oc_ref, ol_ref, yc_s, yl_s):
    chains = [(d, p) for d in range(2) for p in range(SCAN_PAIRS)]

    def run(rm, y0, mm, nn, y_s, states):
        n_chunks = y_s.shape[2] // CHUNK

        def step(t, st):
            out = []
            for (d, p), s in zip(chains, st):
                c = t if d == 0 else n_chunks - 1 - t
                rows = pl.ds(pl.multiple_of(c * CHUNK, CHUNK), CHUNK)
                both = _dot(jnp.concatenate([_embed(rm[d, p, rows, :]), _embed(mm[d, p, rows, :])], axis=0), s)
                y_s[d, p, rows, :] = _fold(both[:LANES]) + y0[d, p, rows, :]
                out.append(both[LANES:] + _embed(nn[d, p, rows, :]))
            return tuple(out)

        return lax.fori_loop(0, n_chunks, step, states)

    zero = jnp.zeros((LANES, LANES), F32)
    states = run(rm_c, y0_c, m_c, n_c, yc_s, tuple(zero for _ in chains))
    run(rm_l, y0_l, m_l, n_l, yl_s, states)

    ones = _head_ones()

    def finish(y_s, bv, g, o_ref):
        def tile(i, carry):
            rows = pl.ds(pl.multiple_of(i * RW_TILE, RW_TILE), RW_TILE)
            for p in range(SCAN_PAIRS):
                y = y_s[0, p, rows, :] + y_s[1, p, rows, :]
                dev = y - _dot_exact_rhs(y, ones, 2) * (1.0 / HEAD_DIM)
                var = _dot_exact_rhs(dev * dev, ones, 2) * (1.0 / HEAD_DIM)
                yn = dev * lax.rsqrt(var + GN_EPS) * vec_ref[p, 7:8, :] + vec_ref[p, 8:9, :]
                o_ref[rows, p * LANES:(p + 1) * LANES] = ((yn + bv[p, rows, :]) * g[p, rows, :]).astype(o_ref.dtype)
            return carry

        lax.fori_loop(0, y_s.shape[2] // RW_TILE, tile, 0)

    finish(yc_s, bv_c, g_c, oc_ref)
    finish(yl_s, bv_l, g_l, ol_ref)


def _rwkv_scan(rm, y0, mm, nn, bv, g, vecs, *, nb, lc, ls):
    pc = SCAN_PAIRS
    ctx_d = pl.BlockSpec((2, pc, lc, LANES), lambda b, j: (0, j, b, 0))
    lat0 = nb * lc // ls
    lat_d = pl.BlockSpec((2, pc, ls, LANES), lambda b, j: (0, j, lat0 + b, 0))
    ctx_1 = pl.BlockSpec((pc, lc, LANES), lambda b, j: (j, b, 0))
    lat_1 = pl.BlockSpec((pc, ls, LANES), lambda b, j: (j, lat0 + b, 0))
    return pl.pallas_call(
        _rwkv_scan_body,
        grid=(nb, N_PAIRS // pc),
        in_specs=[ctx_d, ctx_d, ctx_d, ctx_d, lat_d, lat_d, lat_d, lat_d, ctx_1, ctx_1, lat_1, lat_1,
                  pl.BlockSpec((pc, 16, LANES), lambda b, j: (j, 0, 0))],
        out_specs=[pl.BlockSpec((lc, pc * LANES), lambda b, j: (b, j)),
                   pl.BlockSpec((ls, pc * LANES), lambda b, j: (b, j))],
        out_shape=[jax.ShapeDtypeStruct((nb * lc, GROUP_W), BF16),
                   jax.ShapeDtypeStruct((nb * ls, GROUP_W), BF16)],
        scratch_shapes=[pltpu.VMEM((2, pc, lc, LANES), F32), pltpu.VMEM((2, pc, ls, LANES), F32)],
        compiler_params=_cparams(("parallel", "parallel")),
        name="rwkv7_recurrence_output",
    )(rm, y0, mm, nn, rm, y0, mm, nn, bv, g, bv, g, vecs)


def _gqa_attention(q_pairs, k, v, bias, sinks, stack):
    rows = q_pairs[0].shape[0]
    outs = []
    for first in range(0, 2 * N_PAIRS, stack):
        heads = range(first, first + stack)
        qs = jnp.concatenate([_to_half(q_pairs[h // 2], h % 2, h // N_PAIRS) for h in heads], axis=0)
        s = _dot_nt(qs, k)
        if bias is not None:
            s = s + jnp.tile(bias, (stack, 1))
        sink = None
        if sinks is not None:
            sink = jnp.concatenate([jnp.broadcast_to(sinks[h], (rows, 1)) for h in heads], axis=0)
        o = _softmax_pv(s, v, sink)
        outs += [o[i * rows:(i + 1) * rows] for i in range(stack)]
    return [_merge_heads(outs[2 * p], p // 2, outs[2 * p + 1], p // 2) for p in range(N_PAIRS)]


def _window_body(q_ref, kp_ref, kc_ref, kn_ref, vp_ref, vc_ref, vn_ref, kx_ref, vx_ref,
                 cq_ref, sq_ref, cp_ref, sp_ref, cn_ref, sn_ref, sink_ref, o_ref, *, n_blocks, lc):
    n = pl.program_id(1)
    q = q_ref[...]
    k_lat = jnp.concatenate([_rope(kp_ref[...], cp_ref[...], sp_ref[...]),
                             _rope(kc_ref[...], cq_ref[...], sq_ref[...]),
                             _rope(kn_ref[...], cn_ref[...], sn_ref[...])], axis=0)
    k = jnp.concatenate([kx_ref[...], k_lat], axis=0).astype(BF16)
    v = jnp.concatenate([vx_ref[...], vp_ref[...], vc_ref[...], vn_ref[...]], axis=0).astype(BF16)
    nk = lc + 3 * BLOCK
    qi = lax.broadcasted_iota(jnp.int32, (BLOCK, nk), 0)
    kj = lax.broadcasted_iota(jnp.int32, (BLOCK, nk), 1) - lc
    kblk = n - 1 + kj // BLOCK
    ok = (kj < 0) | ((jnp.abs(kj - BLOCK - qi) <= WINDOW) & (kblk >= 0) & (kblk < n_blocks))
    bias = jnp.where(ok, 0.0, NEG_INF)
    qp = [_rope(q[:, p * LANES:(p + 1) * LANES], cq_ref[...], sq_ref[...]) * Q_SCALE for p in range(N_PAIRS)]
    sinks = [sink_ref[h:h + 1, 0:1] for h in range(2 * N_PAIRS)]
    outs = _gqa_attention(qp, k, v, bias, sinks, stack=4)
    for p in range(N_PAIRS):
        o_ref[:, p * LANES:(p + 1) * LANES] = outs[p].astype(o_ref.dtype)


def _window_attention(qkv, cos, sin, sink, *, nb, lc, ls):
    n_blocks = ls // BLOCK
    base = nb * lc // BLOCK
    kcol, vcol = GROUP_W // LANES, GROUP_W // LANES + 1
    blk = lambda b, n, off: base + b * n_blocks + jnp.clip(n + off, 0, n_blocks - 1)
    kv_spec = lambda col, off: pl.BlockSpec((BLOCK, LANES), lambda b, n: (blk(b, n, off), col))
    rope_spec = lambda off: pl.BlockSpec((BLOCK, LANES), lambda b, n: (jnp.clip(n + off, 0, n_blocks - 1), 0))
    return pl.pallas_call(
        functools.partial(_window_body, n_blocks=n_blocks, lc=lc),
        grid=(nb, n_blocks),
        in_specs=[pl.BlockSpec((BLOCK, GROUP_W), lambda b, n: (blk(b, n, 0), 0)),
                  kv_spec(kcol, -1), kv_spec(kcol, 0), kv_spec(kcol, 1),
                  kv_spec(vcol, -1), kv_spec(vcol, 0), kv_spec(vcol, 1),
                  pl.BlockSpec((lc, LANES), lambda b, n: (b, kcol)),
                  pl.BlockSpec((lc, LANES), lambda b, n: (b, vcol)),
                  rope_spec(0), rope_spec(0), rope_spec(-1), rope_spec(-1), rope_spec(1), rope_spec(1),
                  pl.BlockSpec((8, LANES), lambda b, n: (0, 0))],
        out_specs=pl.BlockSpec((BLOCK, GROUP_W), lambda b, n: (b * n_blocks + n, 0)),
        out_shape=jax.ShapeDtypeStruct((nb * ls, GROUP_W), BF16),
        compiler_params=_cparams(("parallel", "parallel")),
        name="window_attention",
    )(qkv, qkv, qkv, qkv, qkv, qkv, qkv, qkv, qkv, cos, sin, cos, sin, cos, sin, sink)


def _global_body(q_ref, kx_ref, kl_ref, vx_ref, vl_ref, cq_ref, sq_ref, cos_ref, sin_ref, gq_ref, gk_ref,
                 o_ref, k_scr, v_scr, *, lc):
    @pl.when(pl.program_id(1) == 0)
    def _():
        k_scr[0:lc, :] = _head_rms(kx_ref[...], gk_ref[...]).astype(BF16)
        k_scr[lc:, :] = _rope(_head_rms(kl_ref[...], gk_ref[...]), cos_ref[...], sin_ref[...]).astype(BF16)
        v_scr[0:lc, :] = vx_ref[...].astype(BF16)
        v_scr[lc:, :] = vl_ref[...].astype(BF16)

    q = q_ref[...]
    k = k_scr[...]
    v = v_scr[...]
    qp = [_rope(_head_rms(q[:, p * LANES:(p + 1) * LANES], gq_ref[...]), cq_ref[...], sq_ref[...]) * Q_SCALE
          for p in range(N_PAIRS)]
    outs = _gqa_attention(qp, k, v, None, None, stack=2)
    for p in range(N_PAIRS):
        o_ref[:, p * LANES:(p + 1) * LANES] = outs[p].astype(o_ref.dtype)


def _global_attention(qkv, cos, sin, q_gain, k_gain, *, nb, lc, ls):
    n_blocks = ls // BLOCK
    base = nb * lc // BLOCK
    lat0 = nb * lc // ls
    kcol, vcol = GROUP_W // LANES, GROUP_W // LANES + 1
    return pl.pallas_call(
        functools.partial(_global_body, lc=lc),
        grid=(nb, n_blocks),
        in_specs=[pl.BlockSpec((BLOCK, GROUP_W), lambda b, n: (base + b * n_blocks + n, 0)),
                  pl.BlockSpec((lc, LANES), lambda b, n: (b, kcol)),
                  pl.BlockSpec((ls, LANES), lambda b, n: (lat0 + b, kcol)),
                  pl.BlockSpec((lc, LANES), lambda b, n: (b, vcol)),
                  pl.BlockSpec((ls, LANES), lambda b, n: (lat0 + b, vcol)),
                  pl.BlockSpec((BLOCK, LANES), lambda b, n: (n, 0)),
                  pl.BlockSpec((BLOCK, LANES), lambda b, n: (n, 0)),
                  pl.BlockSpec((ls, LANES), lambda b, n: (0, 0)),
                  pl.BlockSpec((ls, LANES), lambda b, n: (0, 0)),
                  pl.BlockSpec((1, LANES), lambda b, n: (0, 0)),
                  pl.BlockSpec((1, LANES), lambda b, n: (0, 0))],
        out_specs=pl.BlockSpec((BLOCK, GROUP_W), lambda b, n: (b * n_blocks + n, 0)),
        out_shape=jax.ShapeDtypeStruct((nb * ls, GROUP_W), BF16),
        scratch_shapes=[pltpu.VMEM((lc + ls, LANES), BF16), pltpu.VMEM((lc + ls, LANES), BF16)],
        compiler_params=_cparams(("parallel", "arbitrary")),
        name="global_attention",
    )(qkv, qkv, qkv, qkv, qkv, cos, sin, cos, sin, q_gain, k_gain)


NA_ROWS = 4


def _na_body(q_ref, kx_ref, kl_ref, vx_ref, vl_ref, bias_ref, o_ref, *, n_rows, lc):
    kh = min(NA_KH, n_rows)
    first = _lane_half((GRID_W, LANES))
    for i in range(NA_ROWS):
        r = pl.program_id(1) * NA_ROWS + i
        rs = jnp.clip(r - kh // 2, 0, n_rows - kh)
        win = pl.ds(pl.multiple_of(rs * GRID_W, GRID_W), kh * GRID_W)
        rows = slice(i * GRID_W, (i + 1) * GRID_W)
        q = q_ref[rows, :] * Q_SCALE
        for p in range(N_PAIRS):
            cols = slice(p * LANES, (p + 1) * LANES)
            k = jnp.concatenate([kx_ref[:, cols], kl_ref[win, cols]], axis=0).astype(BF16)
            v = jnp.concatenate([vx_ref[:, cols], vl_ref[win, cols]], axis=0).astype(BF16)
            s = _dot_nt(_embed(q[:, cols]), k)
            s = s + jnp.concatenate([jnp.zeros((2 * GRID_W, lc), F32), bias_ref[r - rs, p]], axis=1)
            o = _softmax_pv(s, v)
            o_ref[rows, cols] = jnp.where(first, o[:GRID_W], o[GRID_W:]).astype(o_ref.dtype)


def _na_bias_table(rpb, n_rows):
    kh = min(NA_KH, n_rows)
    qc = np.arange(GRID_W)[:, None]
    kc = np.arange(GRID_W)[None, :]
    win_start = np.clip(qc - NA_KW // 2, 0, GRID_W - NA_KW)
    col_ok = (kc >= win_start) & (kc < win_start + NA_KW)
    dc = np.clip(kc - qc + NA_KW - 1, 0, 2 * NA_KW - 2)
    n_dc = 2 * NA_KW - 1
    heads, n_dr = rpb.shape[0], rpb.shape[1]
    onehot = (dc.reshape(1, -1) == np.arange(n_dc)[:, None]).astype(np.float32)
    by_col = jnp.dot(rpb.reshape(heads * n_dr, n_dc), onehot, precision=lax.Precision.HIGHEST)
    by_col = jnp.where(col_ok[None, None], by_col.reshape(heads, n_dr, GRID_W, GRID_W), NEG_INF)
    tab = jnp.stack([by_col[:, NA_KH - 1 - var:NA_KH - 1 - var + kh] for var in range(kh)])
    tab = tab.transpose(0, 1, 3, 2, 4).reshape(kh, N_PAIRS, 2 * GRID_W, kh * GRID_W)
    return (tab * LOG2E).astype(F32)


def _neighbourhood_attention(qkv, bias, *, nb, lc, ls):
    n_rows = ls // GRID_W
    kh = min(NA_KH, n_rows)
    base = nb * lc // GRID_W
    lat0 = nb * lc // ls
    assert n_rows % NA_ROWS == 0
    steps = n_rows // NA_ROWS
    tq = NA_ROWS * GRID_W
    base = nb * lc // tq
    return pl.pallas_call(
        functools.partial(_na_body, n_rows=n_rows, lc=lc),
        grid=(nb, steps),
        in_specs=[pl.BlockSpec((tq, GROUP_W), lambda b, r: (base + b * steps + r, 0)),
                  pl.BlockSpec((lc, GROUP_W), lambda b, r: (b, 1)),
                  pl.BlockSpec((ls, GROUP_W), lambda b, r: (lat0 + b, 1)),
                  pl.BlockSpec((lc, GROUP_W), lambda b, r: (b, 2)),
                  pl.BlockSpec((ls, GROUP_W), lambda b, r: (lat0 + b, 2)),
                  pl.BlockSpec((kh, N_PAIRS, 2 * GRID_W, kh * GRID_W), lambda b, r: (0, 0, 0, 0),
                               pipeline_mode=pl.Buffered(1))],
        out_specs=pl.BlockSpec((tq, GROUP_W), lambda b, r: (b * steps + r, 0)),
        out_shape=jax.ShapeDtypeStruct((nb * ls, GROUP_W), BF16),
        compiler_params=_cparams(("parallel", "parallel")),
        name="neighbourhood_attention",
    )(qkv, qkv, qkv, qkv, qkv, bias)


def _ctx_body(q_ref, k_ref, v_ref, sink_ref, gq_ref, gk_ref, o_ref, *, gqa, use_sink, use_norm):
    q = q_ref[...]
    k = k_ref[...]
    v = v_ref[...].astype(BF16)
    if gqa:
        if use_norm:
            k = _head_rms(k, gk_ref[...])
        k = k.astype(BF16)
        qp = []
        for p in range(N_PAIRS):
            x = q[:, p * LANES:(p + 1) * LANES]
            if use_norm:
                x = _head_rms(x, gq_ref[...])
            qp.append(x * Q_SCALE)
        sinks = [sink_ref[h:h + 1, 0:1] for h in range(2 * N_PAIRS)] if use_sink else None
        outs = _gqa_attention(qp, k, v, None, sinks, stack=4)
        for p in range(N_PAIRS):
            o_ref[:, p * LANES:(p + 1) * LANES] = outs[p].astype(o_ref.dtype)
    else:
        rows = q.shape[0]
        for p in range(N_PAIRS):
            cols = slice(p * LANES, (p + 1) * LANES)
            o = _softmax_pv(_dot_nt(_embed(q[:, cols] * Q_SCALE), k[:, cols]), v[:, cols])
            o_ref[:, cols] = jnp.where(_lane_half((rows, LANES)), o[:rows], o[rows:]).astype(o_ref.dtype)


def _ctx_attention(qkv, *, nb, lc, gqa, q_col=0, sink=None, q_gain=None, k_gain=None):
    kw = KV_W if gqa else GROUP_W
    kcol = (q_col * GROUP_W + GROUP_W) // kw
    dummy = jnp.zeros((8, LANES), F32)
    one = jnp.ones((1, LANES), F32)
    return pl.pallas_call(
        functools.partial(_ctx_body, gqa=gqa, use_sink=sink is not None, use_norm=q_gain is not None),
        grid=(nb,),
        in_specs=[pl.BlockSpec((lc, GROUP_W), lambda b: (b, q_col)),
                  pl.BlockSpec((lc, kw), lambda b: (b, kcol)),
                  pl.BlockSpec((lc, kw), lambda b: (b, kcol + 1)),
                  pl.BlockSpec((8, LANES), lambda b: (0, 0)),
                  pl.BlockSpec((1, LANES), lambda b: (0, 0)),
                  pl.BlockSpec((1, LANES), lambda b: (0, 0))],
        out_specs=pl.BlockSpec((lc, GROUP_W), lambda b: (b, 0)),
        out_shape=jax.ShapeDtypeStruct((nb * lc, GROUP_W), BF16),
        compiler_params=_cparams(("parallel",)),
        name="context_attention",
    )(qkv, qkv, qkv, dummy if sink is None else sink, one if q_gain is None else q_gain,
      one if k_gain is None else k_gain)


def _rope_tables(ls):
    t = jnp.arange(ls, dtype=jnp.int32)
    n_freq = HEAD_DIM // 4
    inv_freq = ROPE_BASE ** (-jnp.arange(n_freq, dtype=F32) / n_freq)
    row = (t // GRID_W).astype(F32)[:, None] * inv_freq
    col = (t % GRID_W).astype(F32)[:, None] * inv_freq
    cos = jnp.concatenate([jnp.cos(row), jnp.cos(row), jnp.cos(col), jnp.cos(col)], axis=-1)
    sin = jnp.concatenate([-jnp.sin(row), jnp.sin(row), -jnp.sin(col), jnp.sin(col)], axis=-1)
    return jnp.tile(cos, (1, 2)), jnp.tile(sin, (1, 2))


def _rwkv_params(mu_prev, mu_next, w0, w2, a0, a2, g2, k_k, k_a, r_k, gn_w, gn_b):
    aw = GROUP_W
    pair = lambda x: x.reshape(N_PAIRS, LANES)

    def mu_rkv(mu):
        return jnp.concatenate([pair(mu[0:aw]), pair(mu[aw:2 * aw]), pair(mu[2 * aw:3 * aw])], axis=-1)

    mu_a = jnp.zeros((N_PAIRS, 8, 3 * LANES), F32)
    mu_a = mu_a.at[:, 0].set(mu_rkv(mu_prev)).at[:, 1].set(mu_rkv(mu_next))
    mu_l = jnp.zeros((8, 3 * LANES), F32).at[0].set(mu_prev[3 * aw:]).at[1].set(mu_next[3 * aw:])
    vecs = jnp.zeros((N_PAIRS, 16, LANES), F32)
    rows = [pair(w0[0]), pair(w0[1]), pair(a0[0]), pair(a0[1]), pair(k_k), pair(k_a), pair(r_k.reshape(-1)),
            pair(gn_w), pair(gn_b)]
    for i, val in enumerate(rows):
        vecs = vecs.at[:, i].set(val)

    def lora_pad(w):
        out = jnp.zeros((2, N_PAIRS, LANES, LANES), F32)
        for d in range(2):
            blk = w[d].reshape(DECAY_LORA, N_PAIRS, LANES).transpose(1, 0, 2)
            out = out.at[d, :, d * DECAY_LORA:(d + 1) * DECAY_LORA, :].set(blk)
        return out

    g2p = g2.reshape(GATE_LORA, N_PAIRS, LANES).transpose(1, 0, 2)
    return dict(mu_rkv=mu_a, mu_lo=mu_l, vecs=vecs, w2=lora_pad(w2), a2=lora_pad(a2), g2=g2p)


def _lane_bcast(x, rows):
    return jnp.zeros((rows, LANES), F32).at[:x.shape[0]].set(jnp.broadcast_to(x[:, None], (x.shape[0], LANES)))


A_W = GROUP_W
A_IN = 3 * A_W + 3 * LANES
B_IN = GROUP_W + 2 * KV_W


def _even_mixer(h, mods, gains, w_in, layer, w_layer, a_prm, sink, cos, sin, *, dims):
    nb, lc, ls = dims["nb"], dims["lc"], dims["ls"]
    t = nb * (lc + ls)
    rkv_dests = [(0, j, part * LANES, part * A_W + j * LANES, LANES)
                 for j in range(N_PAIRS) for part in range(3)]
    pieces = [(0, 3 * A_W, rkv_dests),
              (3 * A_W, 3 * LANES, [(1, None, 0, 0, 3 * LANES)]),
              (A_IN, B_IN, [(2, None, 0, 0, B_IN)])]
    rkv, lo, qkv = _in_proj(h, mods, gains, w_in, layer, w_layer,
                            [(N_PAIRS, t, 3 * LANES), (t, 3 * LANES), (t, B_IN)], pieces,
                            tm=dims["tm"], n_ctx_rows=nb * lc, ls=ls, nb=nb)
    rm, y0, mm, nn, bv, g = _rwkv_precompute(rkv, lo, a_prm, n_ctx_rows=nb * lc, lc=lc, ls=ls)
    a_ctx, a_lat = _rwkv_scan(rm, y0, mm, nn, bv, g, a_prm["vecs"], nb=nb, lc=lc, ls=ls)
    sink_b = _lane_bcast(sink * LOG2E, 8)
    b_lat = _window_attention(qkv, cos, sin, sink_b, nb=nb, lc=lc, ls=ls)
    b_ctx = _ctx_attention(qkv, nb=nb, lc=lc, gqa=True, sink=sink_b)
    return (a_ctx, a_lat), (b_ctx, b_lat)


def _odd_mixer(h, mods, gains, w_in, layer, w_layer, rpb, q_gain, k_gain, cos, sin, *, dims):
    nb, lc, ls = dims["nb"], dims["lc"], dims["ls"]
    t = nb * (lc + ls)
    c_in = 3 * GROUP_W
    pieces = [(0, c_in, [(0, None, 0, 0, c_in)]), (c_in, B_IN, [(1, None, 0, 0, B_IN)])]
    qkv_c, qkv_d = _in_proj(h, mods, gains, w_in, layer, w_layer, [(t, c_in), (t, B_IN)], pieces,
                            tm=dims["tm"], n_ctx_rows=nb * lc, ls=ls, nb=nb)
    bias = _na_bias_table(rpb, ls // GRID_W)
    c_lat = _neighbourhood_attention(qkv_c, bias, nb=nb, lc=lc, ls=ls)
    c_ctx = _ctx_attention(qkv_c, nb=nb, lc=lc, gqa=False)
    gq = jnp.tile(q_gain, 2)[None, :]
    gk = jnp.tile(k_gain, 2)[None, :]
    d_lat = _global_attention(qkv_d, cos, sin, gq, gk, nb=nb, lc=lc, ls=ls)
    d_ctx = _ctx_attention(qkv_d, nb=nb, lc=lc, gqa=True, q_gain=gq, k_gain=gk)
    return (c_ctx, c_lat), (d_ctx, d_lat)


def _row_tile(target, rows_ctx, ls):
    tile = target
    while rows_ctx % tile or ls % tile:
        tile //= 2
    return tile


def kernel(x, c, ctx, c_ctx, w_ada, b_ada, g_pre_mix, g_post_mix, g_pre_ff, g_post_ff, w_in_even, w_in_odd, w_out, w_ff1, w_ff2, a_mu_prev, a_mu_next, a_w0, a_w2, a_a0, a_a2, a_g2, a_k_k, a_k_a, a_r_k, a_gn_w, a_gn_b, b_sink, c_rpb, d_q_gain, d_k_gain):
    nb, ls, d = x.shape
    lc = ctx.shape[1]
    depth = w_ada.shape[0]
    assert nb < 16 and lc % RW_TILE == 0 and ls % RW_TILE == 0 and (nb * lc) % ls == 0
    n_ctx_rows = nb * lc
    tm = _row_tile(512, n_ctx_rows, ls)
    dims = dict(nb=nb, lc=lc, ls=ls, tm=tm)
    cvec = jnp.zeros((16, d), F32).at[:nb].set(c).at[nb].set(c_ctx)
    mods = _modulation(cvec, w_ada, b_ada)
    cos, sin = _rope_tables(ls)
    stack = lambda g: g.reshape(depth, 1, d)
    g_pre_mix, g_post_mix, g_pre_ff, g_post_ff = map(stack, (g_pre_mix, g_post_mix, g_pre_ff, g_post_ff))
    w_in_even, w_in_odd, w_out, w_ff1, w_ff2 = (w.astype(BF16) for w in (w_in_even, w_in_odd, w_out, w_ff1, w_ff2))
    h = (ctx.reshape(n_ctx_rows, d), x.reshape(nb * ls, d))
    for i in range(depth):
        j = i // 2
        last = i == depth - 1
        if i % 2 == 0:
            a_prm = _rwkv_params(a_mu_prev[j], a_mu_next[j], a_w0[j], a_w2[j], a_a0[j], a_a2[j], a_g2[j],
                                 a_k_k[j], a_k_a[j], a_r_k[j], a_gn_w[j], a_gn_b[j])
            mix_a, mix_b = _even_mixer(h, mods, g_pre_mix, w_in_even, i, j, a_prm, b_sink[j], cos, sin, dims=dims)
        else:
            mix_a, mix_b = _odd_mixer(h, mods, g_pre_mix, w_in_odd, i, j, c_rpb[j], d_q_gain[j], d_k_gain[j],
                                      cos, sin, dims=dims)
        h = (_mix_mlp(mix_a, mix_b, h, mods, g_post_mix, g_pre_ff, g_post_ff, w_out, w_ff1, w_ff2, i,
                      tm=tm, n_split=2, n_ctx_rows=n_ctx_rows, ls=ls, nb=nb, first_row=n_ctx_rows if last else 0),)
    return h[0].reshape(nb, ls, d)
```

```python
import functools

import numpy as np
import jax
import jax.numpy as jnp
from jax import lax
from jax.experimental import pallas as pl
from jax.experimental.pallas import tpu as pltpu

F32 = jnp.float32
BF16 = jnp.bfloat16

HEAD_DIM = 64
LANES = 128
N_PAIRS = 4
GROUP_W = N_PAIRS * LANES
KV_W = LANES
DECAY_LORA = 64
GATE_LORA = 128
WINDOW = 128
BLOCK = 128
GRID_W = 64
NA_KH = 8
NA_KW = 16
ROPE_BASE = 10000.0
NORM_EPS = 1e-6
GN_EPS = 64e-5
NEG_INF = -1e30
ATTN_SCALE = HEAD_DIM ** -0.5
LOG2E = 1.4426950408889634
Q_SCALE = ATTN_SCALE * LOG2E
CHUNK = 64
RW_TILE = 256
VMEM_LIMIT = 56 * 1024 * 1024


def _cparams(sem):
    return pltpu.CompilerParams(dimension_semantics=sem, vmem_limit_bytes=VMEM_LIMIT)


def _dot(a, b):
    return jnp.dot(a.astype(BF16), b.astype(BF16), preferred_element_type=F32)


def _dot_nt(a, b):
    return lax.dot_general(a.astype(BF16), b.astype(BF16), (((1,), (1,)), ((), ())),
                           preferred_element_type=F32)


def _dot_tn(a, b):
    return lax.dot_general(a.astype(BF16), b.astype(BF16), (((0,), (0,)), ((), ())),
                           preferred_element_type=F32)


def _dot_f32(a, b):
    return jnp.dot(a, b, preferred_element_type=F32, precision=lax.Precision.HIGHEST)


def _dot_exact_rhs(x, b, pieces):
    b = b.astype(BF16)
    acc = None
    for _ in range(pieces):
        part = x.astype(BF16)
        term = jnp.dot(part, b, preferred_element_type=F32)
        acc = term if acc is None else acc + term
        x = x - part.astype(F32)
    return acc


def _dot_exact_lhs(a, x, pieces):
    a = a.astype(BF16)
    acc = None
    for _ in range(pieces):
        part = x.astype(BF16)
        term = jnp.dot(a, part, preferred_element_type=F32)
        acc = term if acc is None else acc + term
        x = x - part.astype(F32)
    return acc


def _lane_half(shape):
    return lax.broadcasted_iota(jnp.int32, shape, len(shape) - 1) % LANES < HEAD_DIM


def _head_ones():
    r = lax.broadcasted_iota(jnp.int32, (LANES, LANES), 0) // HEAD_DIM
    c = lax.broadcasted_iota(jnp.int32, (LANES, LANES), 1) // HEAD_DIM
    return (r == c).astype(F32)


def _embed(x):
    first = _lane_half(x.shape)
    zero = jnp.zeros_like(x)
    return jnp.concatenate([jnp.where(first, x, zero), jnp.where(first, zero, x)], axis=0)


def _fold(x):
    n = x.shape[0] // 2
    return x[:n] + x[n:]


def _sigmoid(x):
    return 1.0 / (1.0 + jnp.exp(-x))


def _rms_rows(x, gain):
    return x * lax.rsqrt(jnp.mean(x * x, axis=-1, keepdims=True) + NORM_EPS) * gain


def _rope(x, cos, sin):
    lane = lax.broadcasted_iota(jnp.int32, x.shape, 1)
    swapped = jnp.where(lane % 32 < 16, pltpu.roll(x, LANES - 16, 1), pltpu.roll(x, 16, 1))
    return x * cos + swapped * sin


def _head_rms(x, gain):
    ms = _dot_exact_rhs(x * x, _head_ones(), 2) * (1.0 / HEAD_DIM)
    return x * lax.rsqrt(ms + NORM_EPS) * gain


def _to_half(x, src, dst):
    if src != dst:
        x = pltpu.roll(x, HEAD_DIM, 1)
    first = _lane_half(x.shape)
    keep = first if dst == 0 else jnp.logical_not(first)
    return jnp.where(keep, x, jnp.zeros_like(x))


def _softmax_pv(s, v, sink=None):
    m = jnp.max(s, axis=-1, keepdims=True)
    if sink is not None:
        m = jnp.maximum(m, sink)
    p = jnp.exp2(s - m)
    den = jnp.sum(p, axis=-1, keepdims=True)
    if sink is not None:
        den = den + jnp.exp2(sink - m)
    return _dot(p, v) / den


def _merge_heads(o0, src0, o1, src1):
    a = o0 if src0 == 0 else pltpu.roll(o0, HEAD_DIM, 1)
    b = o1 if src1 == 1 else pltpu.roll(o1, HEAD_DIM, 1)
    return jnp.where(_lane_half(a.shape), a, b)


def _mod_body(c_ref, w_ref, b_ref, o_ref):
    c = c_ref[...]
    s = c * _sigmoid(c)
    o_ref[...] = _dot_f32(s, w_ref[...]) + b_ref[...]


def _modulation(cvec, w_ada, b_ada):
    depth, d, _ = w_ada.shape
    out = pl.pallas_call(
        _mod_body,
        grid=(depth, 6),
        in_specs=[pl.BlockSpec((16, d), lambda l, n: (0, 0)),
                  pl.BlockSpec((None, d, d), lambda l, n: (l, 0, n)),
                  pl.BlockSpec((None, None, 1, d), lambda l, n: (l, n, 0, 0))],
        out_specs=pl.BlockSpec((None, None, 16, d), lambda l, n: (l, n, 0, 0)),
        out_shape=jax.ShapeDtypeStruct((depth, 6, 16, d), F32),
        compiler_params=_cparams(("parallel", "parallel")),
        name="adaln_modulation",
    )(cvec, w_ada, b_ada.reshape(depth, 6, 1, d))
    return out.transpose(0, 2, 1, 3)


def _mod_row(tile, tm, n_ctx_rows, ls, nb):
    start = tile * tm
    return jnp.where(start < n_ctx_rows, nb, (start - n_ctx_rows) // ls)


def _row_specs(arrs, tm, n_ctx_tiles, off=0):
    width = arrs[0].shape[1]
    if len(arrs) == 1:
        return [pl.BlockSpec((tm, width), lambda i: (i + off, 0))]
    return [pl.BlockSpec((tm, width), lambda i: (jnp.minimum(i + off, n_ctx_tiles - 1), 0)),
            pl.BlockSpec((tm, width), lambda i: (jnp.maximum(i + off - n_ctx_tiles, 0), 0))]


def _read_rows(refs, n_ctx_tiles, off=0):
    if len(refs) == 1:
        return refs[0][...]
    return jnp.where(pl.program_id(0) + off < n_ctx_tiles, refs[0][...], refs[1][...])


def _inproj_body(*refs, pieces, n_h, n_ctx_tiles):
    x = _read_rows(refs[:n_h], n_ctx_tiles)
    m_ref, g_ref, w_ref = refs[n_h:n_h + 3]
    o_refs = refs[n_h + 3:]
    u = _rms_rows(x, g_ref[...]) * (1.0 + m_ref[1:2, :]) + m_ref[0:1, :]
    u = u.astype(BF16)
    for (src, span, dests) in pieces:
        val = jnp.dot(u, w_ref[:, src:src + span], preferred_element_type=F32)
        for (oi, lead, dst, off, width) in dests:
            if lead is None:
                o_refs[oi][:, dst:dst + width] = val[:, off:off + width]
            else:
                o_refs[oi][lead, :, dst:dst + width] = val[:, off:off + width]


def _in_proj(h, mods, gains, w, layer, w_layer, out_defs, pieces, *, tm, n_ctx_rows, ls, nb):
    t = sum(a.shape[0] for a in h)
    d = h[0].shape[1]
    n_in = w.shape[2]
    nct = n_ctx_rows // tm
    out_shapes, out_specs = [], []
    for shape in out_defs:
        out_shapes.append(jax.ShapeDtypeStruct(shape, F32))
        if len(shape) == 3:
            out_specs.append(pl.BlockSpec((shape[0], tm, shape[2]), lambda i: (0, i, 0)))
        else:
            out_specs.append(pl.BlockSpec((tm, shape[1]), lambda i: (i, 0)))
    row = functools.partial(_mod_row, tm=tm, n_ctx_rows=n_ctx_rows, ls=ls, nb=nb)
    return pl.pallas_call(
        functools.partial(_inproj_body, pieces=pieces, n_h=len(h), n_ctx_tiles=nct),
        grid=(t // tm,),
        in_specs=_row_specs(h, tm, nct) + [
            pl.BlockSpec((None, None, 6, d), lambda i: (layer, row(i), 0, 0)),
            pl.BlockSpec((None, 1, d), lambda i: (layer, 0, 0)),
            pl.BlockSpec((None, d, n_in), lambda i: (w_layer, 0, 0))],
        out_specs=out_specs,
        out_shape=out_shapes,
        compiler_params=_cparams(("parallel",)),
        name="modulate_in_proj",
    )(*h, mods, gains, w)


def _mix_mlp_body(*refs, n_h, n_ctx_tiles, off, n_split):
    a = _read_rows(refs[0:2], n_ctx_tiles, off)
    b = _read_rows(refs[2:4], n_ctx_tiles, off)
    x = _read_rows(refs[4:4 + n_h], n_ctx_tiles, off)
    m_ref, gm_ref, g1_ref, g2_ref, wo_ref, w1_ref, w2_ref, o_ref = refs[4 + n_h:]
    o = (jnp.dot(a, wo_ref[0:GROUP_W, :], preferred_element_type=F32)
         + jnp.dot(b, wo_ref[GROUP_W:2 * GROUP_W, :], preferred_element_type=F32))
    x = x + m_ref[2:3, :] * _rms_rows(o, gm_ref[...])
    u = (_rms_rows(x, g1_ref[...]) * (1.0 + m_ref[4:5, :]) + m_ref[3:4, :]).astype(BF16)
    fc = w1_ref.shape[1] // n_split
    acc = None
    for c in range(n_split):
        f = jnp.dot(u, w1_ref[:, c * fc:(c + 1) * fc], preferred_element_type=F32)
        f = jnp.square(jnp.maximum(f, 0.0)).astype(BF16)
        part = jnp.dot(f, w2_ref[c * fc:(c + 1) * fc, :], preferred_element_type=F32)
        acc = part if acc is None else acc + part
    o_ref[...] = x + m_ref[5:6, :] * _rms_rows(acc, g2_ref[...])


def _mix_mlp(mix_a, mix_b, h, mods, g_post_mix, g_pre_ff, g_post_ff, w_out, w1, w2, layer, *,
             tm, n_split, n_ctx_rows, ls, nb, first_row):
    t = sum(a.shape[0] for a in h)
    d = h[0].shape[1]
    dff = w1.shape[2]
    nct = n_ctx_rows // tm
    off = first_row // tm
    row = functools.partial(_mod_row, tm=tm, n_ctx_rows=n_ctx_rows, ls=ls, nb=nb)
    once = pl.Buffered(1)
    gain = pl.BlockSpec((None, 1, d), lambda i: (layer, 0, 0))
    return pl.pallas_call(
        functools.partial(_mix_mlp_body, n_h=len(h), n_ctx_tiles=nct, off=off, n_split=n_split),
        grid=((t - first_row) // tm,),
        in_specs=(_row_specs(mix_a, tm, nct, off) + _row_specs(mix_b, tm, nct, off) + _row_specs(h, tm, nct, off) + [
            pl.BlockSpec((None, None, 6, d), lambda i: (layer, row(i + off), 0, 0)), gain, gain, gain,
            pl.BlockSpec((None, 2 * GROUP_W, d), lambda i: (layer, 0, 0), pipeline_mode=once),
            pl.BlockSpec((None, d, dff), lambda i: (layer, 0, 0), pipeline_mode=once),
            pl.BlockSpec((None, dff, d), lambda i: (layer, 0, 0), pipeline_mode=once)]),
        out_specs=pl.BlockSpec((tm, d), lambda i: (i, 0)),
        out_shape=jax.ShapeDtypeStruct((t - first_row, d), F32),
        compiler_params=_cparams(("parallel",)),
        name="out_proj_mlp_residuals",
    )(*mix_a, *mix_b, *h, mods, g_post_mix, g_pre_ff, g_post_ff, w_out, w1, w2)


def _rwkv_units(units):
    n = units[0][0].shape[0]
    n2 = 2 * n
    rev = [u[6] for u in units]
    ri = lax.broadcasted_iota(jnp.int32, (n, n), 0)
    ci = lax.broadcasted_iota(jnp.int32, (n, n), 1)
    tri = {False: (ci <= ri).astype(F32), True: (ci >= ri).astype(F32)}
    rr = lax.broadcasted_iota(jnp.int32, (n2, n2), 0)
    cc = lax.broadcasted_iota(jnp.int32, (n2, n2), 1)
    r2, c2 = rr % n, cc % n
    incl2 = {False: c2 <= r2, True: c2 >= r2}
    strict2 = {False: c2 < r2, True: c2 > r2}
    eye = rr == cc
    eye_f = jnp.where(eye, 1.0, 0.0)
    zero = jnp.zeros((n2, n2), F32)
    base = 8
    diag_blk = rr // base == cc // base
    sizes = []
    size = base
    while size < n:
        sizes.append(size)
        size *= 2
    off_blk = [(rr // (2 * s) == cc // (2 * s)) & (rr // s != cc // s) for s in sizes]

    cs = [_dot_exact_lhs(tri[u[6]], u[5], 3) for u in units]
    ops = []
    for (r, v, kd, alpha, beta, logw, _), c in zip(units, cs):
        c_all = jnp.sum(logw, axis=0, keepdims=True)
        e_pos = jnp.exp(c)
        e_neg = jnp.exp(-c)
        e_end = jnp.exp(c_all - c)
        ops.append(dict(rt=_embed(r * e_pos), at=_embed(alpha * jnp.exp(c - logw)), kt=_embed(kd * e_neg),
                        bt=_embed(beta * e_neg), kh=_embed(kd * e_end), bh=_embed(beta * e_end), vb=_embed(v),
                        decay=jnp.exp(c_all)))
    a_all = [_dot_nt(jnp.concatenate([o["rt"], o["at"]], axis=0), jnp.concatenate([o["kt"], o["bt"]], axis=0))
             for o in ops]
    a_qk = [jnp.where(incl2[f], a[:n2, :n2], zero) for a, f in zip(a_all, rev)]
    a_qb = [jnp.where(incl2[f], a[:n2, n2:], zero) for a, f in zip(a_all, rev)]
    a_ak = [jnp.where(strict2[f], a[n2:, :n2], zero) for a, f in zip(a_all, rev)]
    a_ab = [jnp.where(strict2[f], a[n2:, n2:], zero) for a, f in zip(a_all, rev)]
    akv = [_dot(a, o["vb"]) for a, o in zip(a_ak, ops)]
    apow = [jnp.where(diag_blk, a, zero) for a in a_ab]
    inv = [eye_f + a for a in apow]
    d2 = [_dot(a, a) for a in apow]
    both = [_dot(jnp.concatenate([a, x], axis=0), a) for a, x in zip(d2, inv)]
    inv = [x + b[n2:] for x, b in zip(inv, both)]
    inv = [x + _dot(x, b[:n2]) for x, b in zip(inv, both)]
    for blk in off_blk:
        right = [_dot(jnp.where(blk, a, zero), x) for a, x in zip(a_ab, inv)]
        inv = [x + _dot(x, y) for x, y in zip(inv, right)]
    wu = [_dot(x, jnp.concatenate([o["at"], y], axis=1)) for x, o, y in zip(inv, ops, akv)]
    outs = []
    for o, x, qb, qk in zip(ops, wu, a_qb, a_qk):
        w = x[:, :LANES]
        u0v = jnp.concatenate([x[:, LANES:], o["vb"]], axis=0)
        m = jnp.where(eye, o["decay"], 0.0) + _dot_tn(o["bh"], w)
        nn = _dot_tn(jnp.concatenate([o["bh"], o["kh"]], axis=0), u0v)
        rm = o["rt"] + _dot(qb, w)
        y0 = _dot(jnp.concatenate([qb, qk], axis=1), u0v)
        outs.append((_fold(rm), _fold(y0), _fold(m), _fold(nn)))
    return outs


def _shifted(x, prev_row, next_row, mu_prev, mu_next):
    rows = x.shape[0]
    ridx = lax.broadcasted_iota(jnp.int32, x.shape, 0)
    prev = jnp.where(ridx == 0, prev_row, pltpu.roll(x, 1, 0))
    nxt = jnp.where(ridx == rows - 1, next_row, pltpu.roll(x, rows - 1, 0))
    return x + mu_prev * (prev - x) + mu_next * (nxt - x)


PRE_PAIRS = 4


def _rwkv_pre_body(rkv_ref, rkv_p_ref, rkv_n_ref, lo_ref, lo_p_ref, lo_n_ref,
                   mu_rkv_ref, mu_lo_ref, vec_ref, w2_ref, a2_ref, g2_ref,
                   rm_ref, y0_ref, m_ref, n_ref, bv_ref, g_ref,
                   r_s, v_s, kd_s, al_s, be_s, lw_s, *, lat_tiles, ctx_tiles, n_ctx_tiles):
    i = pl.program_id(0)
    lat = i >= n_ctx_tiles
    pos = jnp.where(lat, (i - n_ctx_tiles) % lat_tiles, i % ctx_tiles)
    has_prev = (pos != 0).astype(F32)
    has_next = (pos != jnp.where(lat, lat_tiles, ctx_tiles) - 1).astype(F32)
    lo = _shifted(lo_ref[...], lo_p_ref[7:8, :] * has_prev, lo_n_ref[0:1, :] * has_next,
                  mu_lo_ref[0:1, :], mu_lo_ref[1:2, :])
    wl = jnp.tanh(lo[:, 0:LANES])
    al = lo[:, LANES:2 * LANES]
    gl = _sigmoid(lo[:, 2 * LANES:3 * LANES])
    ones = _head_ones()
    for p in range(PRE_PAIRS):
        rkv = _shifted(rkv_ref[p], rkv_p_ref[p, 7:8, :] * has_prev, rkv_n_ref[p, 0:1, :] * has_next,
                       mu_rkv_ref[p, 0:1, :], mu_rkv_ref[p, 1:2, :])
        r = rkv[:, 0:LANES]
        k = rkv[:, LANES:2 * LANES]
        v = rkv[:, 2 * LANES:3 * LANES]
        kx = k * vec_ref[p, 4:5, :]
        kk = kx / jnp.maximum(jnp.sqrt(_dot_exact_rhs(kx * kx, ones, 2)), 1e-12)
        r_s[p] = r
        v_s[p] = v
        al_s[p] = -kk
        ksum = jnp.zeros_like(k)
        for d in range(2):
            w_raw = vec_ref[p, d:d + 1, :] + _dot(wl, w2_ref[d, p])
            z = -w_raw
            softplus = jnp.maximum(z, 0.0) + jnp.log(1.0 + jnp.exp(-jnp.abs(z)))
            lw_s[p, d] = -jnp.exp(-softplus - 0.5)
            iclr = _sigmoid(vec_ref[p, 2 + d:3 + d, :] + _dot(al, a2_ref[d, p]))
            kd = k * (1.0 + (iclr - 1.0) * vec_ref[p, 5:6, :])
            kd_s[p, d] = kd
            be_s[p, d] = kk * iclr
            ksum = ksum + kd
        bonus = _dot_exact_rhs(r * ksum * vec_ref[p, 6:7, :], ones, 2)
        bv_ref[p] = bonus * v
        g_ref[p] = _dot(gl, g2_ref[p])

    where = [(p, ci, d) for p in range(PRE_PAIRS) for ci in range(RW_TILE // CHUNK) for d in range(2)]
    units = []
    for p, ci, d in where:
        rows = slice(ci * CHUNK, (ci + 1) * CHUNK)
        units.append((r_s[p, rows, :], v_s[p, rows, :], kd_s[p, d, rows, :], al_s[p, rows, :],
                      be_s[p, d, rows, :], lw_s[p, d, rows, :], d == 1))
    outs = _rwkv_units(units)
    for (p, ci, d), (rm, y0, m, nn) in zip(where, outs):
        rows = slice(ci * CHUNK, (ci + 1) * CHUNK)
        rm_ref[d, p, rows, :] = rm.astype(BF16)
        y0_ref[d, p, rows, :] = y0
        m_ref[d, p, rows, :] = m.astype(BF16)
        n_ref[d, p, rows, :] = nn


def _rwkv_precompute(rkv, lo, prm, *, n_ctx_rows, lc, ls):
    _, t, _ = rkv.shape
    n_tiles = t // RW_TILE
    halo = RW_TILE // 8
    last_halo = t // 8 - 1
    pp = PRE_PAIRS
    tile = lambda i, j: (j, i, 0)
    prev = lambda i, j: (j, jnp.maximum(i * halo - 1, 0), 0)
    nxt = lambda i, j: (j, jnp.minimum((i + 1) * halo, last_halo), 0)
    out_dir = jax.ShapeDtypeStruct((2, N_PAIRS, t, LANES), F32)
    out_mat = jax.ShapeDtypeStruct((2, N_PAIRS, t, LANES), BF16)
    out_one = jax.ShapeDtypeStruct((N_PAIRS, t, LANES), F32)
    dir_spec = pl.BlockSpec((2, pp, RW_TILE, LANES), lambda i, j: (0, j, i, 0))
    one_spec = pl.BlockSpec((pp, RW_TILE, LANES), tile)
    body = functools.partial(_rwkv_pre_body, lat_tiles=ls // RW_TILE, ctx_tiles=lc // RW_TILE,
                             n_ctx_tiles=n_ctx_rows // RW_TILE)
    return pl.pallas_call(
        body,
        grid=(n_tiles, N_PAIRS // pp),
        in_specs=[pl.BlockSpec((pp, RW_TILE, 3 * LANES), tile),
                  pl.BlockSpec((pp, 8, 3 * LANES), prev),
                  pl.BlockSpec((pp, 8, 3 * LANES), nxt),
                  pl.BlockSpec((RW_TILE, 3 * LANES), lambda i, j: (i, 0)),
                  pl.BlockSpec((8, 3 * LANES), lambda i, j: (jnp.maximum(i * halo - 1, 0), 0)),
                  pl.BlockSpec((8, 3 * LANES), lambda i, j: (jnp.minimum((i + 1) * halo, last_halo), 0)),
                  pl.BlockSpec((pp, 8, 3 * LANES), lambda i, j: (j, 0, 0)),
                  pl.BlockSpec((8, 3 * LANES), lambda i, j: (0, 0)),
                  pl.BlockSpec((pp, 16, LANES), lambda i, j: (j, 0, 0)),
                  pl.BlockSpec((2, pp, LANES, LANES), lambda i, j: (0, j, 0, 0)),
                  pl.BlockSpec((2, pp, LANES, LANES), lambda i, j: (0, j, 0, 0)),
                  pl.BlockSpec((pp, LANES, LANES), lambda i, j: (j, 0, 0))],
        out_specs=[dir_spec, dir_spec, dir_spec, dir_spec, one_spec, one_spec],
        out_shape=[out_mat, out_dir, out_mat, out_dir, out_one, out_one],
        scratch_shapes=[pltpu.VMEM((pp, RW_TILE, LANES), F32), pltpu.VMEM((pp, RW_TILE, LANES), F32),
                        pltpu.VMEM((pp, 2, RW_TILE, LANES), F32), pltpu.VMEM((pp, RW_TILE, LANES), F32),
                        pltpu.VMEM((pp, 2, RW_TILE, LANES), F32), pltpu.VMEM((pp, 2, RW_TILE, LANES), F32)],
        compiler_params=_cparams(("parallel", "parallel")),
        name="rwkv7_chunk_precompute",
    )(rkv, rkv, rkv, lo, lo, lo, prm["mu_rkv"], prm["mu_lo"], prm["vecs"], prm["w2"], prm["a2"], prm["g2"])


SCAN_PAIRS = 2


def _rwkv_scan_body(rm_c, y0_c, m_c, n_c, rm_l, y0_l, m_l, n_l, bv_c, g_c, bv_l, g_l, vec_ref,
                    oc_ref, ol_ref, yc_s, yl_s):
    chains = [(d, p) for d in range(2) for p in range(SCAN_PAIRS)]

    def run(rm, y0, mm, nn, y_s, states):
        n_chunks = y_s.shape[2] // CHUNK

        def step(t, st):
            out = []
            for (d, p), s in zip(chains, st):
                c = t if d == 0 else n_chunks - 1 - t
                rows = pl.ds(pl.multiple_of(c * CHUNK, CHUNK), CHUNK)
                both = _dot(jnp.concatenate([_embed(rm[d, p, rows, :]), _embed(mm[d, p, rows, :])], axis=0), s)
                y_s[d, p, rows, :] = _fold(both[:LANES]) + y0[d, p, rows, :]
                out.append(both[LANES:] + _embed(nn[d, p, rows, :]))
            return tuple(out)

        return lax.fori_loop(0, n_chunks, step, states)

    zero = jnp.zeros((LANES, LANES), F32)
    states = run(rm_c, y0_c, m_c, n_c, yc_s, tuple(zero for _ in chains))
    run(rm_l, y0_l, m_l, n_l, yl_s, states)

    ones = _head_ones()

    def finish(y_s, bv, g, o_ref):
        def tile(i, carry):
            rows = pl.ds(pl.multiple_of(i * RW_TILE, RW_TILE), RW_TILE)
            for p in range(SCAN_PAIRS):
                y = y_s[0, p, rows, :] + y_s[1, p, rows, :]
                dev = y - _dot_exact_rhs(y, ones, 2) * (1.0 / HEAD_DIM)
                var = _dot_exact_rhs(dev * dev, ones, 2) * (1.0 / HEAD_DIM)
                yn = dev * lax.rsqrt(var + GN_EPS) * vec_ref[p, 7:8, :] + vec_ref[p, 8:9, :]
                o_ref[rows, p * LANES:(p + 1) * LANES] = ((yn + bv[p, rows, :]) * g[p, rows, :]).astype(o_ref.dtype)
            return carry

        lax.fori_loop(0, y_s.shape[2] // RW_TILE, tile, 0)

    finish(yc_s, bv_c, g_c, oc_ref)
    finish(yl_s, bv_l, g_l, ol_ref)


def _rwkv_scan(rm, y0, mm, nn, bv, g, vecs, *, nb, lc, ls):
    pc = SCAN_PAIRS
    ctx_d = pl.BlockSpec((2, pc, lc, LANES), lambda b, j: (0, j, b, 0))
    lat0 = nb * lc // ls
    lat_d = pl.BlockSpec((2, pc, ls, LANES), lambda b, j: (0, j, lat0 + b, 0))
    ctx_1 = pl.BlockSpec((pc, lc, LANES), lambda b, j: (j, b, 0))
    lat_1 = pl.BlockSpec((pc, ls, LANES), lambda b, j: (j, lat0 + b, 0))
    return pl.pallas_call(
        _rwkv_scan_body,
        grid=(nb, N_PAIRS // pc),
        in_specs=[ctx_d, ctx_d, ctx_d, ctx_d, lat_d, lat_d, lat_d, lat_d, ctx_1, ctx_1, lat_1, lat_1,
                  pl.BlockSpec((pc, 16, LANES), lambda b, j: (j, 0, 0))],
        out_specs=[pl.BlockSpec((lc, pc * LANES), lambda b, j: (b, j)),
                   pl.BlockSpec((ls, pc * LANES), lambda b, j: (b, j))],
        out_shape=[jax.ShapeDtypeStruct((nb * lc, GROUP_W), BF16),
                   jax.ShapeDtypeStruct((nb * ls, GROUP_W), BF16)],
        scratch_shapes=[pltpu.VMEM((2, pc, lc, LANES), F32), pltpu.VMEM((2, pc, ls, LANES), F32)],
        compiler_params=_cparams(("parallel", "parallel")),
        name="rwkv7_recurrence_output",
    )(rm, y0, mm, nn, rm, y0, mm, nn, bv, g, bv, g, vecs)


def _gqa_attention(q_pairs, k, v, bias, sinks, stack):
    rows = q_pairs[0].shape[0]
    outs = []
    for first in range(0, 2 * N_PAIRS, stack):
        heads = range(first, first + stack)
        qs = jnp.concatenate([_to_half(q_pairs[h // 2], h % 2, h // N_PAIRS) for h in heads], axis=0)
        s = _dot_nt(qs, k)
        if bias is not None:
            s = s + jnp.tile(bias, (stack, 1))
        sink = None
        if sinks is not None:
            sink = jnp.concatenate([jnp.broadcast_to(sinks[h], (rows, 1)) for h in heads], axis=0)
        o = _softmax_pv(s, v, sink)
        outs += [o[i * rows:(i + 1) * rows] for i in range(stack)]
    return [_merge_heads(outs[2 * p], p // 2, outs[2 * p + 1], p // 2) for p in range(N_PAIRS)]


def _window_body(q_ref, kp_ref, kc_ref, kn_ref, vp_ref, vc_ref, vn_ref, kx_ref, vx_ref,
                 cq_ref, sq_ref, cp_ref, sp_ref, cn_ref, sn_ref, sink_ref, o_ref, *, n_blocks, lc):
    n = pl.program_id(1)
    q = q_ref[...]
    k_lat = jnp.concatenate([_rope(kp_ref[...], cp_ref[...], sp_ref[...]),
                             _rope(kc_ref[...], cq_ref[...], sq_ref[...]),
                             _rope(kn_ref[...], cn_ref[...], sn_ref[...])], axis=0)
    k = jnp.concatenate([kx_ref[...], k_lat], axis=0).astype(BF16)
    v = jnp.concatenate([vx_ref[...], vp_ref[...], vc_ref[...], vn_ref[...]], axis=0).astype(BF16)
    nk = lc + 3 * BLOCK
    qi = lax.broadcasted_iota(jnp.int32, (BLOCK, nk), 0)
    kj = lax.broadcasted_iota(jnp.int32, (BLOCK, nk), 1) - lc
    kblk = n - 1 + kj // BLOCK
    ok = (kj < 0) | ((jnp.abs(kj - BLOCK - qi) <= WINDOW) & (kblk >= 0) & (kblk < n_blocks))
    bias = jnp.where(ok, 0.0, NEG_INF)
    qp = [_rope(q[:, p * LANES:(p + 1) * LANES], cq_ref[...], sq_ref[...]) * Q_SCALE for p in range(N_PAIRS)]
    sinks = [sink_ref[h:h + 1, 0:1] for h in range(2 * N_PAIRS)]
    outs = _gqa_attention(qp, k, v, bias, sinks, stack=4)
    for p in range(N_PAIRS):
        o_ref[:, p * LANES:(p + 1) * LANES] = outs[p].astype(o_ref.dtype)


def _window_attention(qkv, cos, sin, sink, *, nb, lc, ls):
    n_blocks = ls // BLOCK
    base = nb * lc // BLOCK
    kcol, vcol = GROUP_W // LANES, GROUP_W // LANES + 1
    blk = lambda b, n, off: base + b * n_blocks + jnp.clip(n + off, 0, n_blocks - 1)
    kv_spec = lambda col, off: pl.BlockSpec((BLOCK, LANES), lambda b, n: (blk(b, n, off), col))
    rope_spec = lambda off: pl.BlockSpec((BLOCK, LANES), lambda b, n: (jnp.clip(n + off, 0, n_blocks - 1), 0))
    return pl.pallas_call(
        functools.partial(_window_body, n_blocks=n_blocks, lc=lc),
        grid=(nb, n_blocks),
        in_specs=[pl.BlockSpec((BLOCK, GROUP_W), lambda b, n: (blk(b, n, 0), 0)),
                  kv_spec(kcol, -1), kv_spec(kcol, 0), kv_spec(kcol, 1),
                  kv_spec(vcol, -1), kv_spec(vcol, 0), kv_spec(vcol, 1),
                  pl.BlockSpec((lc, LANES), lambda b, n: (b, kcol)),
                  pl.BlockSpec((lc, LANES), lambda b, n: (b, vcol)),
                  rope_spec(0), rope_spec(0), rope_spec(-1), rope_spec(-1), rope_spec(1), rope_spec(1),
                  pl.BlockSpec((8, LANES), lambda b, n: (0, 0))],
        out_specs=pl.BlockSpec((BLOCK, GROUP_W), lambda b, n: (b * n_blocks + n, 0)),
        out_shape=jax.ShapeDtypeStruct((nb * ls, GROUP_W), BF16),
        compiler_params=_cparams(("parallel", "parallel")),
        name="window_attention",
    )(qkv, qkv, qkv, qkv, qkv, qkv, qkv, qkv, qkv, cos, sin, cos, sin, cos, sin, sink)


def _global_body(q_ref, kx_ref, kl_ref, vx_ref, vl_ref, cq_ref, sq_ref, cos_ref, sin_ref, gq_ref, gk_ref,
                 o_ref, k_scr, v_scr, *, lc):
    @pl.when(pl.program_id(1) == 0)
    def _():
        k_scr[0:lc, :] = _head_rms(kx_ref[...], gk_ref[...]).astype(BF16)
        k_scr[lc:, :] = _rope(_head_rms(kl_ref[...], gk_ref[...]), cos_ref[...], sin_ref[...]).astype(BF16)
        v_scr[0:lc, :] = vx_ref[...].astype(BF16)
        v_scr[lc:, :] = vl_ref[...].astype(BF16)

    q = q_ref[...]
    k = k_scr[...]
    v = v_scr[...]
    qp = [_rope(_head_rms(q[:, p * LANES:(p + 1) * LANES], gq_ref[...]), cq_ref[...], sq_ref[...]) * Q_SCALE
          for p in range(N_PAIRS)]
    outs = _gqa_attention(qp, k, v, None, None, stack=2)
    for p in range(N_PAIRS):
        o_ref[:, p * LANES:(p + 1) * LANES] = outs[p].astype(o_ref.dtype)


def _global_attention(qkv, cos, sin, q_gain, k_gain, *, nb, lc, ls):
    n_blocks = ls // BLOCK
    base = nb * lc // BLOCK
    lat0 = nb * lc // ls
    kcol, vcol = GROUP_W // LANES, GROUP_W // LANES + 1
    return pl.pallas_call(
        functools.partial(_global_body, lc=lc),
        grid=(nb, n_blocks),
        in_specs=[pl.BlockSpec((BLOCK, GROUP_W), lambda b, n: (base + b * n_blocks + n, 0)),
                  pl.BlockSpec((lc, LANES), lambda b, n: (b, kcol)),
                  pl.BlockSpec((ls, LANES), lambda b, n: (lat0 + b, kcol)),
                  pl.BlockSpec((lc, LANES), lambda b, n: (b, vcol)),
                  pl.BlockSpec((ls, LANES), lambda b, n: (lat0 + b, vcol)),
                  pl.BlockSpec((BLOCK, LANES), lambda b, n: (n, 0)),
                  pl.BlockSpec((BLOCK, LANES), lambda b, n: (n, 0)),
                  pl.BlockSpec((ls, LANES), lambda b, n: (0, 0)),
                  pl.BlockSpec((ls, LANES), lambda b, n: (0, 0)),
                  pl.BlockSpec((1, LANES), lambda b, n: (0, 0)),
                  pl.BlockSpec((1, LANES), lambda b, n: (0, 0))],
        out_specs=pl.BlockSpec((BLOCK, GROUP_W), lambda b, n: (b * n_blocks + n, 0)),
        out_shape=jax.ShapeDtypeStruct((nb * ls, GROUP_W), BF16),
        scratch_shapes=[pltpu.VMEM((lc + ls, LANES), BF16), pltpu.VMEM((lc + ls, LANES), BF16)],
        compiler_params=_cparams(("parallel", "arbitrary")),
        name="global_attention",
    )(qkv, qkv, qkv, qkv, qkv, cos, sin, cos, sin, q_gain, k_gain)


NA_ROWS = 4


def _na_body(q_ref, kx_ref, kl_ref, vx_ref, vl_ref, bias_ref, o_ref, *, n_rows, lc):
    kh = min(NA_KH, n_rows)
    first = _lane_half((GRID_W, LANES))
    for i in range(NA_ROWS):
        r = pl.program_id(1) * NA_ROWS + i
        rs = jnp.clip(r - kh // 2, 0, n_rows - kh)
        win = pl.ds(pl.multiple_of(rs * GRID_W, GRID_W), kh * GRID_W)
        rows = slice(i * GRID_W, (i + 1) * GRID_W)
        q = q_ref[rows, :] * Q_SCALE
        for p in range(N_PAIRS):
            cols = slice(p * LANES, (p + 1) * LANES)
            k = jnp.concatenate([kx_ref[:, cols], kl_ref[win, cols]], axis=0).astype(BF16)
            v = jnp.concatenate([vx_ref[:, cols], vl_ref[win, cols]], axis=0).astype(BF16)
            s = _dot_nt(_embed(q[:, cols]), k)
            s = s + jnp.concatenate([jnp.zeros((2 * GRID_W, lc), F32), bias_ref[r - rs, p]], axis=1)
            o = _softmax_pv(s, v)
            o_ref[rows, cols] = jnp.where(first, o[:GRID_W], o[GRID_W:]).astype(o_ref.dtype)


def _na_bias_table(rpb, n_rows):
    kh = min(NA_KH, n_rows)
    qc = np.arange(GRID_W)[:, None]
    kc = np.arange(GRID_W)[None, :]
    win_start = np.clip(qc - NA_KW // 2, 0, GRID_W - NA_KW)
    col_ok = (kc >= win_start) & (kc < win_start + NA_KW)
    dc = np.clip(kc - qc + NA_KW - 1, 0, 2 * NA_KW - 2)
    n_dc = 2 * NA_KW - 1
    heads, n_dr = rpb.shape[0], rpb.shape[1]
    onehot = (dc.reshape(1, -1) == np.arange(n_dc)[:, None]).astype(np.float32)
    by_col = jnp.dot(rpb.reshape(heads * n_dr, n_dc), onehot, precision=lax.Precision.HIGHEST)
    by_col = jnp.where(col_ok[None, None], by_col.reshape(heads, n_dr, GRID_W, GRID_W), NEG_INF)
    tab = jnp.stack([by_col[:, NA_KH - 1 - var:NA_KH - 1 - var + kh] for var in range(kh)])
    tab = tab.transpose(0, 1, 3, 2, 4).reshape(kh, N_PAIRS, 2 * GRID_W, kh * GRID_W)
    return (tab * LOG2E).astype(F32)


def _neighbourhood_attention(qkv, bias, *, nb, lc, ls):
    n_rows = ls // GRID_W
    kh = min(NA_KH, n_rows)
    base = nb * lc // GRID_W
    lat0 = nb * lc // ls
    assert n_rows % NA_ROWS == 0
    steps = n_rows // NA_ROWS
    tq = NA_ROWS * GRID_W
    base = nb * lc // tq
    return pl.pallas_call(
        functools.partial(_na_body, n_rows=n_rows, lc=lc),
        grid=(nb, steps),
        in_specs=[pl.BlockSpec((tq, GROUP_W), lambda b, r: (base + b * steps + r, 0)),
                  pl.BlockSpec((lc, GROUP_W), lambda b, r: (b, 1)),
                  pl.BlockSpec((ls, GROUP_W), lambda b, r: (lat0 + b, 1)),
                  pl.BlockSpec((lc, GROUP_W), lambda b, r: (b, 2)),
                  pl.BlockSpec((ls, GROUP_W), lambda b, r: (lat0 + b, 2)),
                  pl.BlockSpec((kh, N_PAIRS, 2 * GRID_W, kh * GRID_W), lambda b, r: (0, 0, 0, 0),
                               pipeline_mode=pl.Buffered(1))],
        out_specs=pl.BlockSpec((tq, GROUP_W), lambda b, r: (b * steps + r, 0)),
        out_shape=jax.ShapeDtypeStruct((nb * ls, GROUP_W), BF16),
        compiler_params=_cparams(("parallel", "parallel")),
        name="neighbourhood_attention",
    )(qkv, qkv, qkv, qkv, qkv, bias)


def _ctx_body(q_ref, k_ref, v_ref, sink_ref, gq_ref, gk_ref, o_ref, *, gqa, use_sink, use_norm):
    q = q_ref[...]
    k = k_ref[...]
    v = v_ref[...].astype(BF16)
    if gqa:
        if use_norm:
            k = _head_rms(k, gk_ref[...])
        k = k.astype(BF16)
        qp = []
        for p in range(N_PAIRS):
            x = q[:, p * LANES:(p + 1) * LANES]
            if use_norm:
                x = _head_rms(x, gq_ref[...])
            qp.append(x * Q_SCALE)
        sinks = [sink_ref[h:h + 1, 0:1] for h in range(2 * N_PAIRS)] if use_sink else None
        outs = _gqa_attention(qp, k, v, None, sinks, stack=4)
        for p in range(N_PAIRS):
            o_ref[:, p * LANES:(p + 1) * LANES] = outs[p].astype(o_ref.dtype)
    else:
        rows = q.shape[0]
        for p in range(N_PAIRS):
            cols = slice(p * LANES, (p + 1) * LANES)
            o = _softmax_pv(_dot_nt(_embed(q[:, cols] * Q_SCALE), k[:, cols]), v[:, cols])
            o_ref[:, cols] = jnp.where(_lane_half((rows, LANES)), o[:rows], o[rows:]).astype(o_ref.dtype)


def _ctx_attention(qkv, *, nb, lc, gqa, q_col=0, sink=None, q_gain=None, k_gain=None):
    kw = KV_W if gqa else GROUP_W
    kcol = (q_col * GROUP_W + GROUP_W) // kw
    dummy = jnp.zeros((8, LANES), F32)
    one = jnp.ones((1, LANES), F32)
    return pl.pallas_call(
        functools.partial(_ctx_body, gqa=gqa, use_sink=sink is not None, use_norm=q_gain is not None),
        grid=(nb,),
        in_specs=[pl.BlockSpec((lc, GROUP_W), lambda b: (b, q_col)),
                  pl.BlockSpec((lc, kw), lambda b: (b, kcol)),
                  pl.BlockSpec((lc, kw), lambda b: (b, kcol + 1)),
                  pl.BlockSpec((8, LANES), lambda b: (0, 0)),
                  pl.BlockSpec((1, LANES), lambda b: (0, 0)),
                  pl.BlockSpec((1, LANES), lambda b: (0, 0))],
        out_specs=pl.BlockSpec((lc, GROUP_W), lambda b: (b, 0)),
        out_shape=jax.ShapeDtypeStruct((nb * lc, GROUP_W), BF16),
        compiler_params=_cparams(("parallel",)),
        name="context_attention",
    )(qkv, qkv, qkv, dummy if sink is None else sink, one if q_gain is None else q_gain,
      one if k_gain is None else k_gain)


def _rope_tables(ls):
    t = jnp.arange(ls, dtype=jnp.int32)
    n_freq = HEAD_DIM // 4
    inv_freq = ROPE_BASE ** (-jnp.arange(n_freq, dtype=F32) / n_freq)
    row = (t // GRID_W).astype(F32)[:, None] * inv_freq
    col = (t % GRID_W).astype(F32)[:, None] * inv_freq
    cos = jnp.concatenate([jnp.cos(row), jnp.cos(row), jnp.cos(col), jnp.cos(col)], axis=-1)
    sin = jnp.concatenate([-jnp.sin(row), jnp.sin(row), -jnp.sin(col), jnp.sin(col)], axis=-1)
    return jnp.tile(cos, (1, 2)), jnp.tile(sin, (1, 2))


def _rwkv_params(mu_prev, mu_next, w0, w2, a0, a2, g2, k_k, k_a, r_k, gn_w, gn_b):
    aw = GROUP_W
    pair = lambda x: x.reshape(N_PAIRS, LANES)

    def mu_rkv(mu):
        return jnp.concatenate([pair(mu[0:aw]), pair(mu[aw:2 * aw]), pair(mu[2 * aw:3 * aw])], axis=-1)

    mu_a = jnp.zeros((N_PAIRS, 8, 3 * LANES), F32)
    mu_a = mu_a.at[:, 0].set(mu_rkv(mu_prev)).at[:, 1].set(mu_rkv(mu_next))
    mu_l = jnp.zeros((8, 3 * LANES), F32).at[0].set(mu_prev[3 * aw:]).at[1].set(mu_next[3 * aw:])
    vecs = jnp.zeros((N_PAIRS, 16, LANES), F32)
    rows = [pair(w0[0]), pair(w0[1]), pair(a0[0]), pair(a0[1]), pair(k_k), pair(k_a), pair(r_k.reshape(-1)),
            pair(gn_w), pair(gn_b)]
    for i, val in enumerate(rows):
        vecs = vecs.at[:, i].set(val)

    def lora_pad(w):
        out = jnp.zeros((2, N_PAIRS, LANES, LANES), F32)
        for d in range(2):
            blk = w[d].reshape(DECAY_LORA, N_PAIRS, LANES).transpose(1, 0, 2)
            out = out.at[d, :, d * DECAY_LORA:(d + 1) * DECAY_LORA, :].set(blk)
        return out

    g2p = g2.reshape(GATE_LORA, N_PAIRS, LANES).transpose(1, 0, 2)
    return dict(mu_rkv=mu_a, mu_lo=mu_l, vecs=vecs, w2=lora_pad(w2), a2=lora_pad(a2), g2=g2p)


def _lane_bcast(x, rows):
    return jnp.zeros((rows, LANES), F32).at[:x.shape[0]].set(jnp.broadcast_to(x[:, None], (x.shape[0], LANES)))


A_W = GROUP_W
A_IN = 3 * A_W + 3 * LANES
B_IN = GROUP_W + 2 * KV_W


def _even_mixer(h, mods, gains, w_in, layer, w_layer, a_prm, sink, cos, sin, *, dims):
    nb, lc, ls = dims["nb"], dims["lc"], dims["ls"]
    t = nb * (lc + ls)
    rkv_dests = [(0, j, part * LANES, part * A_W + j * LANES, LANES)
                 for j in range(N_PAIRS) for part in range(3)]
    pieces = [(0, 3 * A_W, rkv_dests),
              (3 * A_W, 3 * LANES, [(1, None, 0, 0, 3 * LANES)]),
              (A_IN, B_IN, [(2, None, 0, 0, B_IN)])]
    rkv, lo, qkv = _in_proj(h, mods, gains, w_in, layer, w_layer,
                            [(N_PAIRS, t, 3 * LANES), (t, 3 * LANES), (t, B_IN)], pieces,
                            tm=dims["tm"], n_ctx_rows=nb * lc, ls=ls, nb=nb)
    rm, y0, mm, nn, bv, g = _rwkv_precompute(rkv, lo, a_prm, n_ctx_rows=nb * lc, lc=lc, ls=ls)
    a_ctx, a_lat = _rwkv_scan(rm, y0, mm, nn, bv, g, a_prm["vecs"], nb=nb, lc=lc, ls=ls)
    sink_b = _lane_bcast(sink * LOG2E, 8)
    b_lat = _window_attention(qkv, cos, sin, sink_b, nb=nb, lc=lc, ls=ls)
    b_ctx = _ctx_attention(qkv, nb=nb, lc=lc, gqa=True, sink=sink_b)
    return (a_ctx, a_lat), (b_ctx, b_lat)


def _odd_mixer(h, mods, gains, w_in, layer, w_layer, rpb, q_gain, k_gain, cos, sin, *, dims):
    nb, lc, ls = dims["nb"], dims["lc"], dims["ls"]
    t = nb * (lc + ls)
    c_in = 3 * GROUP_W
    pieces = [(0, c_in, [(0, None, 0, 0, c_in)]), (c_in, B_IN, [(1, None, 0, 0, B_IN)])]
    qkv_c, qkv_d = _in_proj(h, mods, gains, w_in, layer, w_layer, [(t, c_in), (t, B_IN)], pieces,
                            tm=dims["tm"], n_ctx_rows=nb * lc, ls=ls, nb=nb)
    bias = _na_bias_table(rpb, ls // GRID_W)
    c_lat = _neighbourhood_attention(qkv_c, bias, nb=nb, lc=lc, ls=ls)
    c_ctx = _ctx_attention(qkv_c, nb=nb, lc=lc, gqa=False)
    gq = jnp.tile(q_gain, 2)[None, :]
    gk = jnp.tile(k_gain, 2)[None, :]
    d_lat = _global_attention(qkv_d, cos, sin, gq, gk, nb=nb, lc=lc, ls=ls)
    d_ctx = _ctx_attention(qkv_d, nb=nb, lc=lc, gqa=True, q_gain=gq, k_gain=gk)
    return (c_ctx, c_lat), (d_ctx, d_lat)


def _row_tile(target, rows_ctx, ls):
    tile = target
    while rows_ctx % tile or ls % tile:
        tile //= 2
    return tile


def kernel(x, c, ctx, c_ctx, w_ada, b_ada, g_pre_mix, g_post_mix, g_pre_ff, g_post_ff, w_in_even, w_in_odd, w_out, w_ff1, w_ff2, a_mu_prev, a_mu_next, a_w0, a_w2, a_a0, a_a2, a_g2, a_k_k, a_k_a, a_r_k, a_gn_w, a_gn_b, b_sink, c_rpb, d_q_gain, d_k_gain):
    nb, ls, d = x.shape
    lc = ctx.shape[1]
    depth = w_ada.shape[0]
    assert nb < 16 and lc % RW_TILE == 0 and ls % RW_TILE == 0 and (nb * lc) % ls == 0
    n_ctx_rows = nb * lc
    tm = _row_tile(512, n_ctx_rows, ls)
    dims = dict(nb=nb, lc=lc, ls=ls, tm=tm)
    cvec = jnp.zeros((16, d), F32).at[:nb].set(c).at[nb].set(c_ctx)
    mods = _modulation(cvec, w_ada, b_ada)
    cos, sin = _rope_tables(ls)
    stack = lambda g: g.reshape(depth, 1, d)
    g_pre_mix, g_post_mix, g_pre_ff, g_post_ff = map(stack, (g_pre_mix, g_post_mix, g_pre_ff, g_post_ff))
    w_in_even, w_in_odd, w_out, w_ff1, w_ff2 = (w.astype(BF16) for w in (w_in_even, w_in_odd, w_out, w_ff1, w_ff2))
    h = (ctx.reshape(n_ctx_rows, d), x.reshape(nb * ls, d))
    for i in range(depth):
        j = i // 2
        last = i == depth - 1
        if i % 2 == 0:
            a_prm = _rwkv_params(a_mu_prev[j], a_mu_next[j], a_w0[j], a_w2[j], a_a0[j], a_a2[j], a_g2[j],
                                 a_k_k[j], a_k_a[j], a_r_k[j], a_gn_w[j], a_gn_b[j])
            mix_a, mix_b = _even_mixer(h, mods, g_pre_mix, w_in_even, i, j, a_prm, b_sink[j], cos, sin, dims=dims)
        else:
            mix_a, mix_b = _odd_mixer(h, mods, g_pre_mix, w_in_odd, i, j, c_rpb[j], d_q_gain[j], d_k_gain[j],
                                      cos, sin, dims=dims)
        h = (_mix_mlp(mix_a, mix_b, h, mods, g_post_mix, g_pre_ff, g_post_ff, w_out, w_ff1, w_ff2, i,
                      tm=tm, n_split=2, n_ctx_rows=n_ctx_rows, ls=ls, nb=nb, first_row=n_ctx_rows if last else 0),)
    return h[0].reshape(nb, ls, d)
```

```python
import functools

import numpy as np
import jax
import jax.numpy as jnp
from jax import lax
from jax.experimental import pallas as pl
from jax.experimental.pallas import tpu as pltpu

F32 = jnp.float32
BF16 = jnp.bfloat16

HEAD_DIM = 64
LANES = 128
N_PAIRS = 4
GROUP_W = N_PAIRS * LANES
KV_W = LANES
DECAY_LORA = 64
GATE_LORA = 128
WINDOW = 128
BLOCK = 128
GRID_W = 64
NA_KH = 8
NA_KW = 16
ROPE_BASE = 10000.0
NORM_EPS = 1e-6
GN_EPS = 64e-5
NEG_INF = -1e30
ATTN_SCALE = HEAD_DIM ** -0.5
LOG2E = 1.4426950408889634
Q_SCALE = ATTN_SCALE * LOG2E
CHUNK = 64
RW_TILE = 256
VMEM_LIMIT = 56 * 1024 * 1024


def _cparams(sem):
    return pltpu.CompilerParams(dimension_semantics=sem, vmem_limit_bytes=VMEM_LIMIT)


def _dot(a, b):
    return jnp.dot(a.astype(BF16), b.astype(BF16), preferred_element_type=F32)


def _dot_nt(a, b):
    return lax.dot_general(a.astype(BF16), b.astype(BF16), (((1,), (1,)), ((), ())),
                           preferred_element_type=F32)


def _dot_tn(a, b):
    return lax.dot_general(a.astype(BF16), b.astype(BF16), (((0,), (0,)), ((), ())),
                           preferred_element_type=F32)


def _dot_f32(a, b):
    return jnp.dot(a, b, preferred_element_type=F32, precision=lax.Precision.HIGHEST)


def _dot_exact_rhs(x, b, pieces):
    b = b.astype(BF16)
    acc = None
    for _ in range(pieces):
        part = x.astype(BF16)
        term = jnp.dot(part, b, preferred_element_type=F32)
        acc = term if acc is None else acc + term
        x = x - part.astype(F32)
    return acc


def _dot_exact_lhs(a, x, pieces):
    a = a.astype(BF16)
    acc = None
    for _ in range(pieces):
        part = x.astype(BF16)
        term = jnp.dot(a, part, preferred_element_type=F32)
        acc = term if acc is None else acc + term
        x = x - part.astype(F32)
    return acc


def _lane_half(shape):
    return lax.broadcasted_iota(jnp.int32, shape, len(shape) - 1) % LANES < HEAD_DIM


def _head_ones():
    r = lax.broadcasted_iota(jnp.int32, (LANES, LANES), 0) // HEAD_DIM
    c = lax.broadcasted_iota(jnp.int32, (LANES, LANES), 1) // HEAD_DIM
    return (r == c).astype(F32)


def _embed(x):
    first = _lane_half(x.shape)
    zero = jnp.zeros_like(x)
    return jnp.concatenate([jnp.where(first, x, zero), jnp.where(first, zero, x)], axis=0)


def _fold(x):
    n = x.shape[0] // 2
    return x[:n] + x[n:]


def _sigmoid(x):
    return 1.0 / (1.0 + jnp.exp(-x))


def _rms_rows(x, gain):
    return x * lax.rsqrt(jnp.mean(x * x, axis=-1, keepdims=True) + NORM_EPS) * gain


def _rope(x, cos, sin):
    lane = lax.broadcasted_iota(jnp.int32, x.shape, 1)
    swapped = jnp.where(lane % 32 < 16, pltpu.roll(x, LANES - 16, 1), pltpu.roll(x, 16, 1))
    return x * cos + swapped * sin


def _head_rms(x, gain):
    ms = _dot_exact_rhs(x * x, _head_ones(), 2) * (1.0 / HEAD_DIM)
    return x * lax.rsqrt(ms + NORM_EPS) * gain


def _to_half(x, src, dst):
    if src != dst:
        x = pltpu.roll(x, HEAD_DIM, 1)
    first = _lane_half(x.shape)
    keep = first if dst == 0 else jnp.logical_not(first)
    return jnp.where(keep, x, jnp.zeros_like(x))


def _softmax_pv(s, v, sink=None):
    m = jnp.max(s, axis=-1, keepdims=True)
    if sink is not None:
        m = jnp.maximum(m, sink)
    p = jnp.exp2(s - m)
    den = jnp.sum(p, axis=-1, keepdims=True)
    if sink is not None:
        den = den + jnp.exp2(sink - m)
    return _dot(p, v) / den


def _merge_heads(o0, src0, o1, src1):
    a = o0 if src0 == 0 else pltpu.roll(o0, HEAD_DIM, 1)
    b = o1 if src1 == 1 else pltpu.roll(o1, HEAD_DIM, 1)
    return jnp.where(_lane_half(a.shape), a, b)


def _mod_body(c_ref, w_ref, b_ref, o_ref):
    c = c_ref[...]
    s = c * _sigmoid(c)
    o_ref[...] = _dot_f32(s, w_ref[...]) + b_ref[...]


def _modulation(cvec, w_ada, b_ada):
    depth, d, _ = w_ada.shape
    out = pl.pallas_call(
        _mod_body,
        grid=(depth, 6),
        in_specs=[pl.BlockSpec((16, d), lambda l, n: (0, 0)),
                  pl.BlockSpec((None, d, d), lambda l, n: (l, 0, n)),
                  pl.BlockSpec((None, None, 1, d), lambda l, n: (l, n, 0, 0))],
        out_specs=pl.BlockSpec((None, None, 16, d), lambda l, n: (l, n, 0, 0)),
        out_shape=jax.ShapeDtypeStruct((depth, 6, 16, d), F32),
        compiler_params=_cparams(("parallel", "parallel")),
        name="adaln_modulation",
    )(cvec, w_ada, b_ada.reshape(depth, 6, 1, d))
    return out.transpose(0, 2, 1, 3)


def _mod_row(tile, tm, n_ctx_rows, ls, nb):
    start = tile * tm
    return jnp.where(start < n_ctx_rows, nb, (start - n_ctx_rows) // ls)


def _row_specs(arrs, tm, n_ctx_tiles, off=0):
    width = arrs[0].shape[1]
    if len(arrs) == 1:
        return [pl.BlockSpec((tm, width), lambda i: (i + off, 0))]
    return [pl.BlockSpec((tm, width), lambda i: (jnp.minimum(i + off, n_ctx_tiles - 1), 0)),
            pl.BlockSpec((tm, width), lambda i: (jnp.maximum(i + off - n_ctx_tiles, 0), 0))]


def _read_rows(refs, n_ctx_tiles, off=0):
    if len(refs) == 1:
        return refs[0][...]
    return jnp.where(pl.program_id(0) + off < n_ctx_tiles, refs[0][...], refs[1][...])


def _inproj_body(*refs, pieces, n_h, n_ctx_tiles):
    x = _read_rows(refs[:n_h], n_ctx_tiles)
    m_ref, g_ref, w_ref = refs[n_h:n_h + 3]
    o_refs = refs[n_h + 3:]
    u = _rms_rows(x, g_ref[...]) * (1.0 + m_ref[1:2, :]) + m_ref[0:1, :]
    u = u.astype(BF16)
    for (src, span, dests) in pieces:
        val = jnp.dot(u, w_ref[:, src:src + span], preferred_element_type=F32)
        for (oi, lead, dst, off, width) in dests:
            if lead is None:
                o_refs[oi][:, dst:dst + width] = val[:, off:off + width]
            else:
                o_refs[oi][lead, :, dst:dst + width] = val[:, off:off + width]


def _in_proj(h, mods, gains, w, layer, w_layer, out_defs, pieces, *, tm, n_ctx_rows, ls, nb):
    t = sum(a.shape[0] for a in h)
    d = h[0].shape[1]
    n_in = w.shape[2]
    nct = n_ctx_rows // tm
    out_shapes, out_specs = [], []
    for shape in out_defs:
        out_shapes.append(jax.ShapeDtypeStruct(shape, F32))
        if len(shape) == 3:
            out_specs.append(pl.BlockSpec((shape[0], tm, shape[2]), lambda i: (0, i, 0)))
        else:
            out_specs.append(pl.BlockSpec((tm, shape[1]), lambda i: (i, 0)))
    row = functools.partial(_mod_row, tm=tm, n_ctx_rows=n_ctx_rows, ls=ls, nb=nb)
    return pl.pallas_call(
        functools.partial(_inproj_body, pieces=pieces, n_h=len(h), n_ctx_tiles=nct),
        grid=(t // tm,),
        in_specs=_row_specs(h, tm, nct) + [
            pl.BlockSpec((None, None, 6, d), lambda i: (layer, row(i), 0, 0)),
            pl.BlockSpec((None, 1, d), lambda i: (layer, 0, 0)),
            pl.BlockSpec((None, d, n_in), lambda i: (w_layer, 0, 0))],
        out_specs=out_specs,
        out_shape=out_shapes,
        compiler_params=_cparams(("parallel",)),
        name="modulate_in_proj",
    )(*h, mods, gains, w)


def _mix_mlp_body(*refs, n_h, n_ctx_tiles, off, n_split):
    a = _read_rows(refs[0:2], n_ctx_tiles, off)
    b = _read_rows(refs[2:4], n_ctx_tiles, off)
    x = _read_rows(refs[4:4 + n_h], n_ctx_tiles, off)
    m_ref, gm_ref, g1_ref, g2_ref, wo_ref, w1_ref, w2_ref, o_ref = refs[4 + n_h:]
    o = (jnp.dot(a, wo_ref[0:GROUP_W, :], preferred_element_type=F32)
         + jnp.dot(b, wo_ref[GROUP_W:2 * GROUP_W, :], preferred_element_type=F32))
    x = x + m_ref[2:3, :] * _rms_rows(o, gm_ref[...])
    u = (_rms_rows(x, g1_ref[...]) * (1.0 + m_ref[4:5, :]) + m_ref[3:4, :]).astype(BF16)
    fc = w1_ref.shape[1] // n_split
    acc = None
    for c in range(n_split):
        f = jnp.dot(u, w1_ref[:, c * fc:(c + 1) * fc], preferred_element_type=F32)
        f = jnp.square(jnp.maximum(f, 0.0)).astype(BF16)
        part = jnp.dot(f, w2_ref[c * fc:(c + 1) * fc, :], preferred_element_type=F32)
        acc = part if acc is None else acc + part
    o_ref[...] = x + m_ref[5:6, :] * _rms_rows(acc, g2_ref[...])


def _mix_mlp(mix_a, mix_b, h, mods, g_post_mix, g_pre_ff, g_post_ff, w_out, w1, w2, layer, *,
             tm, n_split, n_ctx_rows, ls, nb, first_row):
    t = sum(a.shape[0] for a in h)
    d = h[0].shape[1]
    dff = w1.shape[2]
    nct = n_ctx_rows // tm
    off = first_row // tm
    row = functools.partial(_mod_row, tm=tm, n_ctx_rows=n_ctx_rows, ls=ls, nb=nb)
    once = pl.Buffered(1)
    gain = pl.BlockSpec((None, 1, d), lambda i: (layer, 0, 0))
    return pl.pallas_call(
        functools.partial(_mix_mlp_body, n_h=len(h), n_ctx_tiles=nct, off=off, n_split=n_split),
        grid=((t - first_row) // tm,),
        in_specs=(_row_specs(mix_a, tm, nct, off) + _row_specs(mix_b, tm, nct, off) + _row_specs(h, tm, nct, off) + [
            pl.BlockSpec((None, None, 6, d), lambda i: (layer, row(i + off), 0, 0)), gain, gain, gain,
            pl.BlockSpec((None, 2 * GROUP_W, d), lambda i: (layer, 0, 0), pipeline_mode=once),
            pl.BlockSpec((None, d, dff), lambda i: (layer, 0, 0), pipeline_mode=once),
            pl.BlockSpec((None, dff, d), lambda i: (layer, 0, 0), pipeline_mode=once)]),
        out_specs=pl.BlockSpec((tm, d), lambda i: (i, 0)),
        out_shape=jax.ShapeDtypeStruct((t - first_row, d), F32),
        compiler_params=_cparams(("parallel",)),
        name="out_proj_mlp_residuals",
    )(*mix_a, *mix_b, *h, mods, g_post_mix, g_pre_ff, g_post_ff, w_out, w1, w2)


def _rwkv_units(units):
    n = units[0][0].shape[0]
    n2 = 2 * n
    rev = [u[6] for u in units]
    ri = lax.broadcasted_iota(jnp.int32, (n, n), 0)
    ci = lax.broadcasted_iota(jnp.int32, (n, n), 1)
    tri = {False: (ci <= ri).astype(F32), True: (ci >= ri).astype(F32)}
    rr = lax.broadcasted_iota(jnp.int32, (n2, n2), 0)
    cc = lax.broadcasted_iota(jnp.int32, (n2, n2), 1)
    r2, c2 = rr % n, cc % n
    incl2 = {False: c2 <= r2, True: c2 >= r2}
    strict2 = {False: c2 < r2, True: c2 > r2}
    eye = rr == cc
    eye_f = jnp.where(eye, 1.0, 0.0)
    zero = jnp.zeros((n2, n2), F32)
    base = 8
    diag_blk = rr // base == cc // base
    sizes = []
    size = base
    while size < n:
        sizes.append(size)
        size *= 2
    off_blk = [(rr // (2 * s) == cc // (2 * s)) & (rr // s != cc // s) for s in sizes]

    cs = [_dot_exact_lhs(tri[u[6]], u[5], 3) for u in units]
    ops = []
    for (r, v, kd, alpha, beta, logw, _), c in zip(units, cs):
        c_all = jnp.sum(logw, axis=0, keepdims=True)
        e_pos = jnp.exp(c)
        e_neg = jnp.exp(-c)
        e_end = jnp.exp(c_all - c)
        ops.append(dict(rt=_embed(r * e_pos), at=_embed(alpha * jnp.exp(c - logw)), kt=_embed(kd * e_neg),
                        bt=_embed(beta * e_neg), kh=_embed(kd * e_end), bh=_embed(beta * e_end), vb=_embed(v),
                        decay=jnp.exp(c_all)))
    a_all = [_dot_nt(jnp.concatenate([o["rt"], o["at"]], axis=0), jnp.concatenate([o["kt"], o["bt"]], axis=0))
             for o in ops]
    a_qk = [jnp.where(incl2[f], a[:n2, :n2], zero) for a, f in zip(a_all, rev)]
    a_qb = [jnp.where(incl2[f], a[:n2, n2:], zero) for a, f in zip(a_all, rev)]
    a_ak = [jnp.where(strict2[f], a[n2:, :n2], zero) for a, f in zip(a_all, rev)]
    a_ab = [jnp.where(strict2[f], a[n2:, n2:], zero) for a, f in zip(a_all, rev)]
    akv = [_dot(a, o["vb"]) for a, o in zip(a_ak, ops)]
    apow = [jnp.where(diag_blk, a, zero) for a in a_ab]
    inv = [eye_f + a for a in apow]
    d2 = [_dot(a, a) for a in apow]
    both = [_dot(jnp.concatenate([a, x], axis=0), a) for a, x in zip(d2, inv)]
    inv = [x + b[n2:] for x, b in zip(inv, both)]
    inv = [x + _dot(x, b[:n2]) for x, b in zip(inv, both)]
    for blk in off_blk:
        right = [_dot(jnp.where(blk, a, zero), x) for a, x in zip(a_ab, inv)]
        inv = [x + _dot(x, y) for x, y in zip(inv, right)]
    wu = [_dot(x, jnp.concatenate([o["at"], y], axis=1)) for x, o, y in zip(inv, ops, akv)]
    outs = []
    for o, x, qb, qk in zip(ops, wu, a_qb, a_qk):
        w = x[:, :LANES]
        u0v = jnp.concatenate([x[:, LANES:], o["vb"]], axis=0)
        m = jnp.where(eye, o["decay"], 0.0) + _dot_tn(o["bh"], w)
        nn = _dot_tn(jnp.concatenate([o["bh"], o["kh"]], axis=0), u0v)
        rm = o["rt"] + _dot(qb, w)
        y0 = _dot(jnp.concatenate([qb, qk], axis=1), u0v)
        outs.append((_fold(rm), _fold(y0), _fold(m), _fold(nn)))
    return outs


def _shifted(x, prev_row, next_row, mu_prev, mu_next):
    rows = x.shape[0]
    ridx = lax.broadcasted_iota(jnp.int32, x.shape, 0)
    prev = jnp.where(ridx == 0, prev_row, pltpu.roll(x, 1, 0))
    nxt = jnp.where(ridx == rows - 1, next_row, pltpu.roll(x, rows - 1, 0))
    return x + mu_prev * (prev - x) + mu_next * (nxt - x)


PRE_PAIRS = 4


def _rwkv_pre_body(rkv_ref, rkv_p_ref, rkv_n_ref, lo_ref, lo_p_ref, lo_n_ref,
                   mu_rkv_ref, mu_lo_ref, vec_ref, w2_ref, a2_ref, g2_ref,
                   rm_ref, y0_ref, m_ref, n_ref, bv_ref, g_ref,
                   r_s, v_s, kd_s, al_s, be_s, lw_s, *, lat_tiles, ctx_tiles, n_ctx_tiles):
    i = pl.program_id(0)
    lat = i >= n_ctx_tiles
    pos = jnp.where(lat, (i - n_ctx_tiles) % lat_tiles, i % ctx_tiles)
    has_prev = (pos != 0).astype(F32)
    has_next = (pos != jnp.where(lat, lat_tiles, ctx_tiles) - 1).astype(F32)
    lo = _shifted(lo_ref[...], lo_p_ref[7:8, :] * has_prev, lo_n_ref[0:1, :] * has_next,
                  mu_lo_ref[0:1, :], mu_lo_ref[1:2, :])
    wl = jnp.tanh(lo[:, 0:LANES])
    al = lo[:, LANES:2 * LANES]
    gl = _sigmoid(lo[:, 2 * LANES:3 * LANES])
    ones = _head_ones()
    for p in range(PRE_PAIRS):
        rkv = _shifted(rkv_ref[p], rkv_p_ref[p, 7:8, :] * has_prev, rkv_n_ref[p, 0:1, :] * has_next,
                       mu_rkv_ref[p, 0:1, :], mu_rkv_ref[p, 1:2, :])
        r = rkv[:, 0:LANES]
        k = rkv[:, LANES:2 * LANES]
        v = rkv[:, 2 * LANES:3 * LANES]
        kx = k * vec_ref[p, 4:5, :]
        kk = kx / jnp.maximum(jnp.sqrt(_dot_exact_rhs(kx * kx, ones, 2)), 1e-12)
        r_s[p] = r
        v_s[p] = v
        al_s[p] = -kk
        ksum = jnp.zeros_like(k)
        for d in range(2):
            w_raw = vec_ref[p, d:d + 1, :] + _dot(wl, w2_ref[d, p])
            z = -w_raw
            softplus = jnp.maximum(z, 0.0) + jnp.log(1.0 + jnp.exp(-jnp.abs(z)))
            lw_s[p, d] = -jnp.exp(-softplus - 0.5)
            iclr = _sigmoid(vec_ref[p, 2 + d:3 + d, :] + _dot(al, a2_ref[d, p]))
            kd = k * (1.0 + (iclr - 1.0) * vec_ref[p, 5:6, :])
            kd_s[p, d] = kd
            be_s[p, d] = kk * iclr
            ksum = ksum + kd
        bonus = _dot_exact_rhs(r * ksum * vec_ref[p, 6:7, :], ones, 2)
        bv_ref[p] = bonus * v
        g_ref[p] = _dot(gl, g2_ref[p])

    where = [(p, ci, d) for p in range(PRE_PAIRS) for ci in range(RW_TILE // CHUNK) for d in range(2)]
    units = []
    for p, ci, d in where:
        rows = slice(ci * CHUNK, (ci + 1) * CHUNK)
        units.append((r_s[p, rows, :], v_s[p, rows, :], kd_s[p, d, rows, :], al_s[p, rows, :],
                      be_s[p, d, rows, :], lw_s[p, d, rows, :], d == 1))
    outs = _rwkv_units(units)
    for (p, ci, d), (rm, y0, m, nn) in zip(where, outs):
        rows = slice(ci * CHUNK, (ci + 1) * CHUNK)
        rm_ref[d, p, rows, :] = rm.astype(BF16)
        y0_ref[d, p, rows, :] = y0
        m_ref[d, p, rows, :] = m.astype(BF16)
        n_ref[d, p, rows, :] = nn


def _rwkv_precompute(rkv, lo, prm, *, n_ctx_rows, lc, ls):
    _, t, _ = rkv.shape
    n_tiles = t // RW_TILE
    halo = RW_TILE // 8
    last_halo = t // 8 - 1
    pp = PRE_PAIRS
    tile = lambda i, j: (j, i, 0)
    prev = lambda i, j: (j, jnp.maximum(i * halo - 1, 0), 0)
    nxt = lambda i, j: (j, jnp.minimum((i + 1) * halo, last_halo), 0)
    out_dir = jax.ShapeDtypeStruct((2, N_PAIRS, t, LANES), F32)
    out_mat = jax.ShapeDtypeStruct((2, N_PAIRS, t, LANES), BF16)
    out_one = jax.ShapeDtypeStruct((N_PAIRS, t, LANES), F32)
    dir_spec = pl.BlockSpec((2, pp, RW_TILE, LANES), lambda i, j: (0, j, i, 0))
    one_spec = pl.BlockSpec((pp, RW_TILE, LANES), tile)
    body = functools.partial(_rwkv_pre_body, lat_tiles=ls // RW_TILE, ctx_tiles=lc // RW_TILE,
                             n_ctx_tiles=n_ctx_rows // RW_TILE)
    return pl.pallas_call(
        body,
        grid=(n_tiles, N_PAIRS // pp),
        in_specs=[pl.BlockSpec((pp, RW_TILE, 3 * LANES), tile),
                  pl.BlockSpec((pp, 8, 3 * LANES), prev),
                  pl.BlockSpec((pp, 8, 3 * LANES), nxt),
                  pl.BlockSpec((RW_TILE, 3 * LANES), lambda i, j: (i, 0)),
                  pl.BlockSpec((8, 3 * LANES), lambda i, j: (jnp.maximum(i * halo - 1, 0), 0)),
                  pl.BlockSpec((8, 3 * LANES), lambda i, j: (jnp.minimum((i + 1) * halo, last_halo), 0)),
                  pl.BlockSpec((pp, 8, 3 * LANES), lambda i, j: (j, 0, 0)),
                  pl.BlockSpec((8, 3 * LANES), lambda i, j: (0, 0)),
                  pl.BlockSpec((pp, 16, LANES), lambda i, j: (j, 0, 0)),
                  pl.BlockSpec((2, pp, LANES, LANES), lambda i, j: (0, j, 0, 0)),
                  pl.BlockSpec((2, pp, LANES, LANES), lambda i, j: (0, j, 0, 0)),
                  pl.BlockSpec((pp, LANES, LANES), lambda i, j: (j, 0, 0))],
        out_specs=[dir_spec, dir_spec, dir_spec, dir_spec, one_spec, one_spec],
        out_shape=[out_mat, out_dir, out_mat, out_dir, out_one, out_one],
        scratch_shapes=[pltpu.VMEM((pp, RW_TILE, LANES), F32), pltpu.VMEM((pp, RW_TILE, LANES), F32),
                        pltpu.VMEM((pp, 2, RW_TILE, LANES), F32), pltpu.VMEM((pp, RW_TILE, LANES), F32),
                        pltpu.VMEM((pp, 2, RW_TILE, LANES), F32), pltpu.VMEM((pp, 2, RW_TILE, LANES), F32)],
        compiler_params=_cparams(("parallel", "parallel")),
        name="rwkv7_chunk_precompute",
    )(rkv, rkv, rkv, lo, lo, lo, prm["mu_rkv"], prm["mu_lo"], prm["vecs"], prm["w2"], prm["a2"], prm["g2"])


SCAN_PAIRS = 2


def _rwkv_scan_body(rm_c, y0_c, m_c, n_c, rm_l, y0_l, m_l, n_l, bv_c, g_c, bv_l, g_l, vec_ref,
                    oc_ref, ol_ref, yc_s, yl_s):
    chains = [(d, p) for d in range(2) for p in range(SCAN_PAIRS)]

    def run(rm, y0, mm, nn, y_s, states):
        n_chunks = y_s.shape[2] // CHUNK

        def step(t, st):
            out = []
            for (d, p), s in zip(chains, st):
                c = t if d == 0 else n_chunks - 1 - t
                rows = pl.ds(pl.multiple_of(c * CHUNK, CHUNK), CHUNK)
                both = _dot(jnp.concatenate([_embed(rm[d, p, rows, :]), _embed(mm[d, p, rows, :])], axis=0), s)
                y_s[d, p, rows, :] = _fold(both[:LANES]) + y0[d, p, rows, :]
                out.append(both[LANES:] + _embed(nn[d, p, rows, :]))
            return tuple(out)

        return lax.fori_loop(0, n_chunks, step, states)

    zero = jnp.zeros((LANES, LANES), F32)
    states = run(rm_c, y0_c, m_c, n_c, yc_s, tuple(zero for _ in chains))
    run(rm_l, y0_l, m_l, n_l, yl_s, states)

    ones = _head_ones()

    def finish(y_s, bv, g, o_ref):
        def tile(i, carry):
            rows = pl.ds(pl.multiple_of(i * RW_TILE, RW_TILE), RW_TILE)
            for p in range(SCAN_PAIRS):
                y = y_s[0, p, rows, :] + y_s[1, p, rows, :]
                dev = y - _dot_exact_rhs(y, ones, 2) * (1.0 / HEAD_DIM)
                var = _dot_exact_rhs(dev * dev, ones, 2) * (1.0 / HEAD_DIM)
                yn = dev * lax.rsqrt(var + GN_EPS) * vec_ref[p, 7:8, :] + vec_ref[p, 8:9, :]
                o_ref[rows, p * LANES:(p + 1) * LANES] = ((yn + bv[p, rows, :]) * g[p, rows, :]).astype(o_ref.dtype)
            return carry

        lax.fori_loop(0, y_s.shape[2] // RW_TILE, tile, 0)

    finish(yc_s, bv_c, g_c, oc_ref)
    finish(yl_s, bv_l, g_l, ol_ref)


def _rwkv_scan(rm, y0, mm, nn, bv, g, vecs, *, nb, lc, ls):
    pc = SCAN_PAIRS
    ctx_d = pl.BlockSpec((2, pc, lc, LANES), lambda b, j: (0, j, b, 0))
    lat0 = nb * lc // ls
    lat_d = pl.BlockSpec((2, pc, ls, LANES), lambda b, j: (0, j, lat0 + b, 0))
    ctx_1 = pl.BlockSpec((pc, lc, LANES), lambda b, j: (j, b, 0))
    lat_1 = pl.BlockSpec((pc, ls, LANES), lambda b, j: (j, lat0 + b, 0))
    return pl.pallas_call(
        _rwkv_scan_body,
        grid=(nb, N_PAIRS // pc),
        in_specs=[ctx_d, ctx_d, ctx_d, ctx_d, lat_d, lat_d, lat_d, lat_d, ctx_1, ctx_1, lat_1, lat_1,
                  pl.BlockSpec((pc, 16, LANES), lambda b, j: (j, 0, 0))],
        out_specs=[pl.BlockSpec((lc, pc * LANES), lambda b, j: (b, j)),
                   pl.BlockSpec((ls, pc * LANES), lambda b, j: (b, j))],
        out_shape=[jax.ShapeDtypeStruct((nb * lc, GROUP_W), BF16),
                   jax.ShapeDtypeStruct((nb * ls, GROUP_W), BF16)],
        scratch_shapes=[pltpu.VMEM((2, pc, lc, LANES), F32), pltpu.VMEM((2, pc, ls, LANES), F32)],
        compiler_params=_cparams(("parallel", "parallel")),
        name="rwkv7_recurrence_output",
    )(rm, y0, mm, nn, rm, y0, mm, nn, bv, g, bv, g, vecs)


def _gqa_attention(q_pairs, k, v, bias, sinks, stack):
    rows = q_pairs[0].shape[0]
    outs = []
    for first in range(0, 2 * N_PAIRS, stack):
        heads = range(first, first + stack)
        qs = jnp.concatenate([_to_half(q_pairs[h // 2], h % 2, h // N_PAIRS) for h in heads], axis=0)
        s = _dot_nt(qs, k)
        if bias is not None:
            s = s + jnp.tile(bias, (stack, 1))
        sink = None
        if sinks is not None:
            sink = jnp.concatenate([jnp.broadcast_to(sinks[h], (rows, 1)) for h in heads], axis=0)
        o = _softmax_pv(s, v, sink)
        outs += [o[i * rows:(i + 1) * rows] for i in range(stack)]
    return [_merge_heads(outs[2 * p], p // 2, outs[2 * p + 1], p // 2) for p in range(N_PAIRS)]


WINDOW_QB = 4


def _window_body(q_ref, kx_ref, kl_ref, vx_ref, vl_ref, cq_ref, sq_ref, cos_ref, sin_ref, sink_ref,
                 o_ref, k_scr, v_scr, *, lc, ls):
    @pl.when(pl.program_id(1) == 0)
    def _():
        k_scr[0:lc, :] = kx_ref[...].astype(BF16)
        k_scr[lc:, :] = _rope(kl_ref[...], cos_ref[...], sin_ref[...]).astype(BF16)
        v_scr[0:lc, :] = vx_ref[...].astype(BF16)
        v_scr[lc:, :] = vl_ref[...].astype(BF16)

    span = 3 * BLOCK
    nk = lc + span
    qi = lax.broadcasted_iota(jnp.int32, (BLOCK, nk), 0)
    kj = lax.broadcasted_iota(jnp.int32, (BLOCK, nk), 1) - lc
    sinks = [sink_ref[h:h + 1, 0:1] for h in range(2 * N_PAIRS)]
    for i in range(WINDOW_QB):
        rows = slice(i * BLOCK, (i + 1) * BLOCK)
        q0 = (pl.program_id(1) * WINDOW_QB + i) * BLOCK
        k0 = pl.multiple_of(jnp.clip(q0 - BLOCK, 0, ls - span), BLOCK)
        k = jnp.concatenate([k_scr[0:lc, :], k_scr[pl.ds(lc + k0, span), :]], axis=0)
        v = jnp.concatenate([v_scr[0:lc, :], v_scr[pl.ds(lc + k0, span), :]], axis=0)
        ok = (kj < 0) | (jnp.abs((k0 + kj) - (q0 + qi)) <= WINDOW)
        bias = jnp.where(ok, 0.0, NEG_INF)
        q = q_ref[rows, :]
        qp = [_rope(q[:, p * LANES:(p + 1) * LANES], cq_ref[rows, :], sq_ref[rows, :]) * Q_SCALE
              for p in range(N_PAIRS)]
        outs = _gqa_attention(qp, k, v, bias, sinks, stack=4)
        for p in range(N_PAIRS):
            o_ref[rows, p * LANES:(p + 1) * LANES] = outs[p].astype(o_ref.dtype)


def _window_attention(qkv, cos, sin, sink, *, nb, lc, ls):
    tq = WINDOW_QB * BLOCK
    n_steps = ls // tq
    base = nb * lc // tq
    lat0 = nb * lc // ls
    kcol, vcol = GROUP_W // LANES, GROUP_W // LANES + 1
    return pl.pallas_call(
        functools.partial(_window_body, lc=lc, ls=ls),
        grid=(nb, n_steps),
        in_specs=[pl.BlockSpec((tq, GROUP_W), lambda b, n: (base + b * n_steps + n, 0)),
                  pl.BlockSpec((lc, LANES), lambda b, n: (b, kcol)),
                  pl.BlockSpec((ls, LANES), lambda b, n: (lat0 + b, kcol)),
                  pl.BlockSpec((lc, LANES), lambda b, n: (b, vcol)),
                  pl.BlockSpec((ls, LANES), lambda b, n: (lat0 + b, vcol)),
                  pl.BlockSpec((tq, LANES), lambda b, n: (n, 0)),
                  pl.BlockSpec((tq, LANES), lambda b, n: (n, 0)),
                  pl.BlockSpec((ls, LANES), lambda b, n: (0, 0)),
                  pl.BlockSpec((ls, LANES), lambda b, n: (0, 0)),
                  pl.BlockSpec((8, LANES), lambda b, n: (0, 0))],
        out_specs=pl.BlockSpec((tq, GROUP_W), lambda b, n: (b * n_steps + n, 0)),
        out_shape=jax.ShapeDtypeStruct((nb * ls, GROUP_W), BF16),
        scratch_shapes=[pltpu.VMEM((lc + ls, LANES), BF16), pltpu.VMEM((lc + ls, LANES), BF16)],
        compiler_params=_cparams(("parallel", "arbitrary")),
        name="window_attention",
    )(qkv, qkv, qkv, qkv, qkv, cos, sin, cos, sin, sink)


def _global_body(q_ref, kx_ref, kl_ref, vx_ref, vl_ref, cq_ref, sq_ref, cos_ref, sin_ref, gq_ref, gk_ref,
                 o_ref, k_scr, v_scr, *, lc):
    @pl.when(pl.program_id(1) == 0)
    def _():
        k_scr[0:lc, :] = _head_rms(kx_ref[...], gk_ref[...]).astype(BF16)
        k_scr[lc:, :] = _rope(_head_rms(kl_ref[...], gk_ref[...]), cos_ref[...], sin_ref[...]).astype(BF16)
        v_scr[0:lc, :] = vx_ref[...].astype(BF16)
        v_scr[lc:, :] = vl_ref[...].astype(BF16)

    k = k_scr[...]
    v = v_scr[...]
    for i in range(GLOBAL_QB):
        rows = slice(i * BLOCK, (i + 1) * BLOCK)
        q = q_ref[rows, :]
        qp = [_rope(_head_rms(q[:, p * LANES:(p + 1) * LANES], gq_ref[...]), cq_ref[rows, :], sq_ref[rows, :])
              * Q_SCALE for p in range(N_PAIRS)]
        outs = _gqa_attention(qp, k, v, None, None, stack=2)
        for p in range(N_PAIRS):
            o_ref[rows, p * LANES:(p + 1) * LANES] = outs[p].astype(o_ref.dtype)


GLOBAL_QB = 4


def _global_attention(qkv, cos, sin, q_gain, k_gain, *, nb, lc, ls):
    tq = GLOBAL_QB * BLOCK
    n_blocks = ls // tq
    base = nb * lc // tq
    lat0 = nb * lc // ls
    kcol, vcol = GROUP_W // LANES, GROUP_W // LANES + 1
    return pl.pallas_call(
        functools.partial(_global_body, lc=lc),
        grid=(nb, n_blocks),
        in_specs=[pl.BlockSpec((tq, GROUP_W), lambda b, n: (base + b * n_blocks + n, 0)),
                  pl.BlockSpec((lc, LANES), lambda b, n: (b, kcol)),
                  pl.BlockSpec((ls, LANES), lambda b, n: (lat0 + b, kcol)),
                  pl.BlockSpec((lc, LANES), lambda b, n: (b, vcol)),
                  pl.BlockSpec((ls, LANES), lambda b, n: (lat0 + b, vcol)),
                  pl.BlockSpec((tq, LANES), lambda b, n: (n, 0)),
                  pl.BlockSpec((tq, LANES), lambda b, n: (n, 0)),
                  pl.BlockSpec((ls, LANES), lambda b, n: (0, 0)),
                  pl.BlockSpec((ls, LANES), lambda b, n: (0, 0)),
                  pl.BlockSpec((1, LANES), lambda b, n: (0, 0)),
                  pl.BlockSpec((1, LANES), lambda b, n: (0, 0))],
        out_specs=pl.BlockSpec((tq, GROUP_W), lambda b, n: (b * n_blocks + n, 0)),
        out_shape=jax.ShapeDtypeStruct((nb * ls, GROUP_W), BF16),
        scratch_shapes=[pltpu.VMEM((lc + ls, LANES), BF16), pltpu.VMEM((lc + ls, LANES), BF16)],
        compiler_params=_cparams(("parallel", "arbitrary")),
        name="global_attention",
    )(qkv, qkv, qkv, qkv, qkv, cos, sin, cos, sin, q_gain, k_gain)


NA_ROWS = 4


def _na_body(q_ref, kx_ref, kl_ref, vx_ref, vl_ref, bias_ref, o_ref, *, n_rows, lc):
    kh = min(NA_KH, n_rows)
    first = _lane_half((GRID_W, LANES))
    for i in range(NA_ROWS):
        r = pl.program_id(1) * NA_ROWS + i
        rs = jnp.clip(r - kh // 2, 0, n_rows - kh)
        win = pl.ds(pl.multiple_of(rs * GRID_W, GRID_W), kh * GRID_W)
        rows = slice(i * GRID_W, (i + 1) * GRID_W)
        q = q_ref[rows, :] * Q_SCALE
        for p in range(N_PAIRS):
            cols = slice(p * LANES, (p + 1) * LANES)
            k = jnp.concatenate([kx_ref[:, cols], kl_ref[win, cols]], axis=0).astype(BF16)
            v = jnp.concatenate([vx_ref[:, cols], vl_ref[win, cols]], axis=0).astype(BF16)
            s = _dot_nt(_embed(q[:, cols]), k)
            s = s + jnp.concatenate([jnp.zeros((2 * GRID_W, lc), F32), bias_ref[r - rs, p]], axis=1)
            o = _softmax_pv(s, v)
            o_ref[rows, cols] = jnp.where(first, o[:GRID_W], o[GRID_W:]).astype(o_ref.dtype)


def _na_bias_table(rpb, n_rows):
    kh = min(NA_KH, n_rows)
    qc = np.arange(GRID_W)[:, None]
    kc = np.arange(GRID_W)[None, :]
    win_start = np.clip(qc - NA_KW // 2, 0, GRID_W - NA_KW)
    col_ok = (kc >= win_start) & (kc < win_start + NA_KW)
    dc = np.clip(kc - qc + NA_KW - 1, 0, 2 * NA_KW - 2)
    n_dc = 2 * NA_KW - 1
    heads, n_dr = rpb.shape[0], rpb.shape[1]
    onehot = (dc.reshape(1, -1) == np.arange(n_dc)[:, None]).astype(np.float32)
    by_col = jnp.dot(rpb.reshape(heads * n_dr, n_dc), onehot, precision=lax.Precision.HIGHEST)
    by_col = jnp.where(col_ok[None, None], by_col.reshape(heads, n_dr, GRID_W, GRID_W), NEG_INF)
    tab = jnp.stack([by_col[:, NA_KH - 1 - var:NA_KH - 1 - var + kh] for var in range(kh)])
    tab = tab.transpose(0, 1, 3, 2, 4).reshape(kh, N_PAIRS, 2 * GRID_W, kh * GRID_W)
    return (tab * LOG2E).astype(F32)


def _neighbourhood_attention(qkv, bias, *, nb, lc, ls):
    n_rows = ls // GRID_W
    kh = min(NA_KH, n_rows)
    base = nb * lc // GRID_W
    lat0 = nb * lc // ls
    assert n_rows % NA_ROWS == 0
    steps = n_rows // NA_ROWS
    tq = NA_ROWS * GRID_W
    base = nb * lc // tq
    return pl.pallas_call(
        functools.partial(_na_body, n_rows=n_rows, lc=lc),
        grid=(nb, steps),
        in_specs=[pl.BlockSpec((tq, GROUP_W), lambda b, r: (base + b * steps + r, 0)),
                  pl.BlockSpec((lc, GROUP_W), lambda b, r: (b, 1)),
                  pl.BlockSpec((ls, GROUP_W), lambda b, r: (lat0 + b, 1)),
                  pl.BlockSpec((lc, GROUP_W), lambda b, r: (b, 2)),
                  pl.BlockSpec((ls, GROUP_W), lambda b, r: (lat0 + b, 2)),
                  pl.BlockSpec((kh, N_PAIRS, 2 * GRID_W, kh * GRID_W), lambda b, r: (0, 0, 0, 0),
                               pipeline_mode=pl.Buffered(1))],
        out_specs=pl.BlockSpec((tq, GROUP_W), lambda b, r: (b * steps + r, 0)),
        out_shape=jax.ShapeDtypeStruct((nb * ls, GROUP_W), BF16),
        compiler_params=_cparams(("parallel", "parallel")),
        name="neighbourhood_attention",
    )(qkv, qkv, qkv, qkv, qkv, bias)


def _ctx_body(q_ref, k_ref, v_ref, sink_ref, gq_ref, gk_ref, o_ref, *, gqa, use_sink, use_norm):
    q = q_ref[...]
    k = k_ref[...]
    v = v_ref[...].astype(BF16)
    if gqa:
        if use_norm:
            k = _head_rms(k, gk_ref[...])
        k = k.astype(BF16)
        qp = []
        for p in range(N_PAIRS):
            x = q[:, p * LANES:(p + 1) * LANES]
            if use_norm:
                x = _head_rms(x, gq_ref[...])
            qp.append(x * Q_SCALE)
        sinks = [sink_ref[h:h + 1, 0:1] for h in range(2 * N_PAIRS)] if use_sink else None
        outs = _gqa_attention(qp, k, v, None, sinks, stack=4)
        for p in range(N_PAIRS):
            o_ref[:, p * LANES:(p + 1) * LANES] = outs[p].astype(o_ref.dtype)
    else:
        rows = q.shape[0]
        for p in range(N_PAIRS):
            cols = slice(p * LANES, (p + 1) * LANES)
            o = _softmax_pv(_dot_nt(_embed(q[:, cols] * Q_SCALE), k[:, cols]), v[:, cols])
            o_ref[:, cols] = jnp.where(_lane_half((rows, LANES)), o[:rows], o[rows:]).astype(o_ref.dtype)


def _ctx_attention(qkv, *, nb, lc, gqa, q_col=0, sink=None, q_gain=None, k_gain=None):
    kw = KV_W if gqa else GROUP_W
    kcol = (q_col * GROUP_W + GROUP_W) // kw
    dummy = jnp.zeros((8, LANES), F32)
    one = jnp.ones((1, LANES), F32)
    return pl.pallas_call(
        functools.partial(_ctx_body, gqa=gqa, use_sink=sink is not None, use_norm=q_gain is not None),
        grid=(nb,),
        in_specs=[pl.BlockSpec((lc, GROUP_W), lambda b: (b, q_col)),
                  pl.BlockSpec((lc, kw), lambda b: (b, kcol)),
                  pl.BlockSpec((lc, kw), lambda b: (b, kcol + 1)),
                  pl.BlockSpec((8, LANES), lambda b: (0, 0)),
                  pl.BlockSpec((1, LANES), lambda b: (0, 0)),
                  pl.BlockSpec((1, LANES), lambda b: (0, 0))],
        out_specs=pl.BlockSpec((lc, GROUP_W), lambda b: (b, 0)),
        out_shape=jax.ShapeDtypeStruct((nb * lc, GROUP_W), BF16),
        compiler_params=_cparams(("parallel",)),
        name="context_attention",
    )(qkv, qkv, qkv, dummy if sink is None else sink, one if q_gain is None else q_gain,
      one if k_gain is None else k_gain)


def _rope_tables(ls):
    t = jnp.arange(ls, dtype=jnp.int32)
    n_freq = HEAD_DIM // 4
    inv_freq = ROPE_BASE ** (-jnp.arange(n_freq, dtype=F32) / n_freq)
    row = (t // GRID_W).astype(F32)[:, None] * inv_freq
    col = (t % GRID_W).astype(F32)[:, None] * inv_freq
    cos = jnp.concatenate([jnp.cos(row), jnp.cos(row), jnp.cos(col), jnp.cos(col)], axis=-1)
    sin = jnp.concatenate([-jnp.sin(row), jnp.sin(row), -jnp.sin(col), jnp.sin(col)], axis=-1)
    return jnp.tile(cos, (1, 2)), jnp.tile(sin, (1, 2))


def _rwkv_params(mu_prev, mu_next, w0, w2, a0, a2, g2, k_k, k_a, r_k, gn_w, gn_b):
    aw = GROUP_W
    pair = lambda x: x.reshape(N_PAIRS, LANES)

    def mu_rkv(mu):
        return jnp.concatenate([pair(mu[0:aw]), pair(mu[aw:2 * aw]), pair(mu[2 * aw:3 * aw])], axis=-1)

    mu_a = jnp.zeros((N_PAIRS, 8, 3 * LANES), F32)
    mu_a = mu_a.at[:, 0].set(mu_rkv(mu_prev)).at[:, 1].set(mu_rkv(mu_next))
    mu_l = jnp.zeros((8, 3 * LANES), F32).at[0].set(mu_prev[3 * aw:]).at[1].set(mu_next[3 * aw:])
    vecs = jnp.zeros((N_PAIRS, 16, LANES), F32)
    rows = [pair(w0[0]), pair(w0[1]), pair(a0[0]), pair(a0[1]), pair(k_k), pair(k_a), pair(r_k.reshape(-1)),
            pair(gn_w), pair(gn_b)]
    for i, val in enumerate(rows):
        vecs = vecs.at[:, i].set(val)

    def lora_pad(w):
        out = jnp.zeros((2, N_PAIRS, LANES, LANES), F32)
        for d in range(2):
            blk = w[d].reshape(DECAY_LORA, N_PAIRS, LANES).transpose(1, 0, 2)
            out = out.at[d, :, d * DECAY_LORA:(d + 1) * DECAY_LORA, :].set(blk)
        return out

    g2p = g2.reshape(GATE_LORA, N_PAIRS, LANES).transpose(1, 0, 2)
    return dict(mu_rkv=mu_a, mu_lo=mu_l, vecs=vecs, w2=lora_pad(w2), a2=lora_pad(a2), g2=g2p)


def _lane_bcast(x, rows):
    return jnp.zeros((rows, LANES), F32).at[:x.shape[0]].set(jnp.broadcast_to(x[:, None], (x.shape[0], LANES)))


A_W = GROUP_W
A_IN = 3 * A_W + 3 * LANES
B_IN = GROUP_W + 2 * KV_W


def _even_mixer(h, mods, gains, w_in, layer, w_layer, a_prm, sink, cos, sin, *, dims):
    nb, lc, ls = dims["nb"], dims["lc"], dims["ls"]
    t = nb * (lc + ls)
    rkv_dests = [(0, j, part * LANES, part * A_W + j * LANES, LANES)
                 for j in range(N_PAIRS) for part in range(3)]
    pieces = [(0, 3 * A_W, rkv_dests),
              (3 * A_W, 3 * LANES, [(1, None, 0, 0, 3 * LANES)]),
              (A_IN, B_IN, [(2, None, 0, 0, B_IN)])]
    rkv, lo, qkv = _in_proj(h, mods, gains, w_in, layer, w_layer,
                            [(N_PAIRS, t, 3 * LANES), (t, 3 * LANES), (t, B_IN)], pieces,
                            tm=dims["tm"], n_ctx_rows=nb * lc, ls=ls, nb=nb)
    rm, y0, mm, nn, bv, g = _rwkv_precompute(rkv, lo, a_prm, n_ctx_rows=nb * lc, lc=lc, ls=ls)
    a_ctx, a_lat = _rwkv_scan(rm, y0, mm, nn, bv, g, a_prm["vecs"], nb=nb, lc=lc, ls=ls)
    sink_b = _lane_bcast(sink * LOG2E, 8)
    b_lat = _window_attention(qkv, cos, sin, sink_b, nb=nb, lc=lc, ls=ls)
    b_ctx = _ctx_attention(qkv, nb=nb, lc=lc, gqa=True, sink=sink_b)
    return (a_ctx, a_lat), (b_ctx, b_lat)


def _odd_mixer(h, mods, gains, w_in, layer, w_layer, rpb, q_gain, k_gain, cos, sin, *, dims):
    nb, lc, ls = dims["nb"], dims["lc"], dims["ls"]
    t = nb * (lc + ls)
    c_in = 3 * GROUP_W
    pieces = [(0, c_in, [(0, None, 0, 0, c_in)]), (c_in, B_IN, [(1, None, 0, 0, B_IN)])]
    qkv_c, qkv_d = _in_proj(h, mods, gains, w_in, layer, w_layer, [(t, c_in), (t, B_IN)], pieces,
                            tm=dims["tm"], n_ctx_rows=nb * lc, ls=ls, nb=nb)
    bias = _na_bias_table(rpb, ls // GRID_W)
    c_lat = _neighbourhood_attention(qkv_c, bias, nb=nb, lc=lc, ls=ls)
    c_ctx = _ctx_attention(qkv_c, nb=nb, lc=lc, gqa=False)
    gq = jnp.tile(q_gain, 2)[None, :]
    gk = jnp.tile(k_gain, 2)[None, :]
    d_lat = _global_attention(qkv_d, cos, sin, gq, gk, nb=nb, lc=lc, ls=ls)
    d_ctx = _ctx_attention(qkv_d, nb=nb, lc=lc, gqa=True, q_gain=gq, k_gain=gk)
    return (c_ctx, c_lat), (d_ctx, d_lat)


def _row_tile(target, rows_ctx, ls):
    tile = target
    while rows_ctx % tile or ls % tile:
        tile //= 2
    return tile


def kernel(x, c, ctx, c_ctx, w_ada, b_ada, g_pre_mix, g_post_mix, g_pre_ff, g_post_ff, w_in_even, w_in_odd, w_out, w_ff1, w_ff2, a_mu_prev, a_mu_next, a_w0, a_w2, a_a0, a_a2, a_g2, a_k_k, a_k_a, a_r_k, a_gn_w, a_gn_b, b_sink, c_rpb, d_q_gain, d_k_gain):
    nb, ls, d = x.shape
    lc = ctx.shape[1]
    depth = w_ada.shape[0]
    assert nb < 16 and lc % RW_TILE == 0 and ls % RW_TILE == 0 and (nb * lc) % ls == 0
    n_ctx_rows = nb * lc
    tm = _row_tile(512, n_ctx_rows, ls)
    dims = dict(nb=nb, lc=lc, ls=ls, tm=tm)
    cvec = jnp.zeros((16, d), F32).at[:nb].set(c).at[nb].set(c_ctx)
    mods = _modulation(cvec, w_ada, b_ada)
    cos, sin = _rope_tables(ls)
    stack = lambda g: g.reshape(depth, 1, d)
    g_pre_mix, g_post_mix, g_pre_ff, g_post_ff = map(stack, (g_pre_mix, g_post_mix, g_pre_ff, g_post_ff))
    w_in_even, w_in_odd, w_out, w_ff1, w_ff2 = (w.astype(BF16) for w in (w_in_even, w_in_odd, w_out, w_ff1, w_ff2))
    h = (ctx.reshape(n_ctx_rows, d), x.reshape(nb * ls, d))
    for i in range(depth):
        j = i // 2
        last = i == depth - 1
        if i % 2 == 0:
            a_prm = _rwkv_params(a_mu_prev[j], a_mu_next[j], a_w0[j], a_w2[j], a_a0[j], a_a2[j], a_g2[j],
                                 a_k_k[j], a_k_a[j], a_r_k[j], a_gn_w[j], a_gn_b[j])
            mix_a, mix_b = _even_mixer(h, mods, g_pre_mix, w_in_even, i, j, a_prm, b_sink[j], cos, sin, dims=dims)
        else:
            mix_a, mix_b = _odd_mixer(h, mods, g_pre_mix, w_in_odd, i, j, c_rpb[j], d_q_gain[j], d_k_gain[j],
                                      cos, sin, dims=dims)
        h = (_mix_mlp(mix_a, mix_b, h, mods, g_post_mix, g_pre_ff, g_post_ff, w_out, w_ff1, w_ff2, i,
                      tm=tm, n_split=2, n_ctx_rows=n_ctx_rows, ls=ls, nb=nb, first_row=n_ctx_rows if last else 0),)
    return h[0].reshape(nb, ls, d)
```

```python
import functools

import numpy as np
import jax
import jax.numpy as jnp
from jax import lax
from jax.experimental import pallas as pl
from jax.experimental.pallas import tpu as pltpu

F32 = jnp.float32
BF16 = jnp.bfloat16

HEAD_DIM = 64
LANES = 128
N_PAIRS = 4
GROUP_W = N_PAIRS * LANES
KV_W = LANES
DECAY_LORA = 64
GATE_LORA = 128
WINDOW = 128
BLOCK = 128
GRID_W = 64
NA_KH = 8
NA_KW = 16
ROPE_BASE = 10000.0
NORM_EPS = 1e-6
GN_EPS = 64e-5
NEG_INF = -1e30
ATTN_SCALE = HEAD_DIM ** -0.5
LOG2E = 1.4426950408889634
Q_SCALE = ATTN_SCALE * LOG2E
CHUNK = 64
RW_TILE = 256
VMEM_LIMIT = 56 * 1024 * 1024


def _cparams(sem):
    return pltpu.CompilerParams(dimension_semantics=sem, vmem_limit_bytes=VMEM_LIMIT)


def _dot(a, b):
    return jnp.dot(a.astype(BF16), b.astype(BF16), preferred_element_type=F32)


def _dot_nt(a, b):
    return lax.dot_general(a.astype(BF16), b.astype(BF16), (((1,), (1,)), ((), ())),
                           preferred_element_type=F32)


def _dot_tn(a, b):
    return lax.dot_general(a.astype(BF16), b.astype(BF16), (((0,), (0,)), ((), ())),
                           preferred_element_type=F32)


def _dot_f32(a, b):
    return jnp.dot(a, b, preferred_element_type=F32, precision=lax.Precision.HIGHEST)


def _dot_exact_rhs(x, b, pieces):
    b = b.astype(BF16)
    acc = None
    for _ in range(pieces):
        part = x.astype(BF16)
        term = jnp.dot(part, b, preferred_element_type=F32)
        acc = term if acc is None else acc + term
        x = x - part.astype(F32)
    return acc


def _dot_exact_lhs(a, x, pieces):
    a = a.astype(BF16)
    acc = None
    for _ in range(pieces):
        part = x.astype(BF16)
        term = jnp.dot(a, part, preferred_element_type=F32)
        acc = term if acc is None else acc + term
        x = x - part.astype(F32)
    return acc


def _lane_half(shape):
    return lax.broadcasted_iota(jnp.int32, shape, len(shape) - 1) % LANES < HEAD_DIM


def _head_ones():
    r = lax.broadcasted_iota(jnp.int32, (LANES, LANES), 0) // HEAD_DIM
    c = lax.broadcasted_iota(jnp.int32, (LANES, LANES), 1) // HEAD_DIM
    return (r == c).astype(F32)


def _embed(x):
    first = _lane_half(x.shape)
    zero = jnp.zeros_like(x)
    return jnp.concatenate([jnp.where(first, x, zero), jnp.where(first, zero, x)], axis=0)


def _fold(x):
    n = x.shape[0] // 2
    return x[:n] + x[n:]


def _sigmoid(x):
    return 1.0 / (1.0 + jnp.exp(-x))


def _rms_rows(x, gain):
    return x * lax.rsqrt(jnp.mean(x * x, axis=-1, keepdims=True) + NORM_EPS) * gain


def _rope(x, cos, sin):
    lane = lax.broadcasted_iota(jnp.int32, x.shape, 1)
    swapped = jnp.where(lane % 32 < 16, pltpu.roll(x, LANES - 16, 1), pltpu.roll(x, 16, 1))
    return x * cos + swapped * sin


def _head_rms(x, gain):
    ms = _dot_exact_rhs(x * x, _head_ones(), 2) * (1.0 / HEAD_DIM)
    return x * lax.rsqrt(ms + NORM_EPS) * gain


def _to_half(x, src, dst):
    if src != dst:
        x = pltpu.roll(x, HEAD_DIM, 1)
    first = _lane_half(x.shape)
    keep = first if dst == 0 else jnp.logical_not(first)
    return jnp.where(keep, x, jnp.zeros_like(x))


def _softmax_pv(s, v, sink=None):
    m = jnp.max(s, axis=-1, keepdims=True)
    if sink is not None:
        m = jnp.maximum(m, sink)
    p = jnp.exp2(s - m)
    den = jnp.sum(p, axis=-1, keepdims=True)
    if sink is not None:
        den = den + jnp.exp2(sink - m)
    return _dot(p, v) / den


def _merge_heads(o0, src0, o1, src1):
    a = o0 if src0 == 0 else pltpu.roll(o0, HEAD_DIM, 1)
    b = o1 if src1 == 1 else pltpu.roll(o1, HEAD_DIM, 1)
    return jnp.where(_lane_half(a.shape), a, b)


def _mod_body(c_ref, w_ref, b_ref, o_ref):
    c = c_ref[...]
    s = c * _sigmoid(c)
    o_ref[...] = _dot_f32(s, w_ref[...]) + b_ref[...]


def _modulation(cvec, w_ada, b_ada):
    depth, d, _ = w_ada.shape
    out = pl.pallas_call(
        _mod_body,
        grid=(depth, 6),
        in_specs=[pl.BlockSpec((16, d), lambda l, n: (0, 0)),
                  pl.BlockSpec((None, d, d), lambda l, n: (l, 0, n)),
                  pl.BlockSpec((None, None, 1, d), lambda l, n: (l, n, 0, 0))],
        out_specs=pl.BlockSpec((None, None, 16, d), lambda l, n: (l, n, 0, 0)),
        out_shape=jax.ShapeDtypeStruct((depth, 6, 16, d), F32),
        compiler_params=_cparams(("parallel", "parallel")),
        name="adaln_modulation",
    )(cvec, w_ada, b_ada.reshape(depth, 6, 1, d))
    return out.transpose(0, 2, 1, 3)


def _mod_row(tile, tm, n_ctx_rows, ls, nb):
    start = tile * tm
    return jnp.where(start < n_ctx_rows, nb, (start - n_ctx_rows) // ls)


def _row_specs(arrs, tm, n_ctx_tiles, off=0):
    width = arrs[0].shape[1]
    if len(arrs) == 1:
        return [pl.BlockSpec((tm, width), lambda i: (i + off, 0))]
    return [pl.BlockSpec((tm, width), lambda i: (jnp.minimum(i + off, n_ctx_tiles - 1), 0)),
            pl.BlockSpec((tm, width), lambda i: (jnp.maximum(i + off - n_ctx_tiles, 0), 0))]


def _read_rows(refs, n_ctx_tiles, off=0):
    if len(refs) == 1:
        return refs[0][...]
    return jnp.where(pl.program_id(0) + off < n_ctx_tiles, refs[0][...], refs[1][...])


def _inproj_body(*refs, pieces, n_h, n_ctx_tiles):
    x = _read_rows(refs[:n_h], n_ctx_tiles)
    m_ref, g_ref, w_ref = refs[n_h:n_h + 3]
    o_refs = refs[n_h + 3:]
    u = _rms_rows(x, g_ref[...]) * (1.0 + m_ref[1:2, :]) + m_ref[0:1, :]
    u = u.astype(BF16)
    for (src, span, dests) in pieces:
        val = jnp.dot(u, w_ref[:, src:src + span], preferred_element_type=F32)
        for (oi, lead, dst, off, width) in dests:
            if lead is None:
                o_refs[oi][:, dst:dst + width] = val[:, off:off + width]
            else:
                o_refs[oi][lead, :, dst:dst + width] = val[:, off:off + width]


def _in_proj(h, mods, gains, w, layer, w_layer, out_defs, pieces, *, tm, n_ctx_rows, ls, nb):
    t = sum(a.shape[0] for a in h)
    d = h[0].shape[1]
    n_in = w.shape[2]
    nct = n_ctx_rows // tm
    out_shapes, out_specs = [], []
    for shape in out_defs:
        out_shapes.append(jax.ShapeDtypeStruct(shape, F32))
        if len(shape) == 3:
            out_specs.append(pl.BlockSpec((shape[0], tm, shape[2]), lambda i: (0, i, 0)))
        else:
            out_specs.append(pl.BlockSpec((tm, shape[1]), lambda i: (i, 0)))
    row = functools.partial(_mod_row, tm=tm, n_ctx_rows=n_ctx_rows, ls=ls, nb=nb)
    return pl.pallas_call(
        functools.partial(_inproj_body, pieces=pieces, n_h=len(h), n_ctx_tiles=nct),
        grid=(t // tm,),
        in_specs=_row_specs(h, tm, nct) + [
            pl.BlockSpec((None, None, 6, d), lambda i: (layer, row(i), 0, 0)),
            pl.BlockSpec((None, 1, d), lambda i: (layer, 0, 0)),
            pl.BlockSpec((None, d, n_in), lambda i: (w_layer, 0, 0))],
        out_specs=out_specs,
        out_shape=out_shapes,
        compiler_params=_cparams(("parallel",)),
        name="modulate_in_proj",
    )(*h, mods, gains, w)


def _mix_mlp_body(*refs, n_h, n_ctx_tiles, off, n_split):
    a = _read_rows(refs[0:2], n_ctx_tiles, off)
    b = _read_rows(refs[2:4], n_ctx_tiles, off)
    x = _read_rows(refs[4:4 + n_h], n_ctx_tiles, off)
    m_ref, gm_ref, g1_ref, g2_ref, wo_ref, w1_ref, w2_ref, o_ref = refs[4 + n_h:]
    o = (jnp.dot(a, wo_ref[0:GROUP_W, :], preferred_element_type=F32)
         + jnp.dot(b, wo_ref[GROUP_W:2 * GROUP_W, :], preferred_element_type=F32))
    x = x + m_ref[2:3, :] * _rms_rows(o, gm_ref[...])
    u = (_rms_rows(x, g1_ref[...]) * (1.0 + m_ref[4:5, :]) + m_ref[3:4, :]).astype(BF16)
    fc = w1_ref.shape[1] // n_split
    acc = None
    for c in range(n_split):
        f = jnp.dot(u, w1_ref[:, c * fc:(c + 1) * fc], preferred_element_type=F32)
        f = jnp.square(jnp.maximum(f, 0.0)).astype(BF16)
        part = jnp.dot(f, w2_ref[c * fc:(c + 1) * fc, :], preferred_element_type=F32)
        acc = part if acc is None else acc + part
    o_ref[...] = x + m_ref[5:6, :] * _rms_rows(acc, g2_ref[...])


def _mix_mlp(mix_a, mix_b, h, mods, g_post_mix, g_pre_ff, g_post_ff, w_out, w1, w2, layer, *,
             tm, n_split, n_ctx_rows, ls, nb, first_row):
    t = sum(a.shape[0] for a in h)
    d = h[0].shape[1]
    dff = w1.shape[2]
    nct = n_ctx_rows // tm
    off = first_row // tm
    row = functools.partial(_mod_row, tm=tm, n_ctx_rows=n_ctx_rows, ls=ls, nb=nb)
    once = pl.Buffered(1)
    gain = pl.BlockSpec((None, 1, d), lambda i: (layer, 0, 0))
    return pl.pallas_call(
        functools.partial(_mix_mlp_body, n_h=len(h), n_ctx_tiles=nct, off=off, n_split=n_split),
        grid=((t - first_row) // tm,),
        in_specs=(_row_specs(mix_a, tm, nct, off) + _row_specs(mix_b, tm, nct, off) + _row_specs(h, tm, nct, off) + [
            pl.BlockSpec((None, None, 6, d), lambda i: (layer, row(i + off), 0, 0)), gain, gain, gain,
            pl.BlockSpec((None, 2 * GROUP_W, d), lambda i: (layer, 0, 0), pipeline_mode=once),
            pl.BlockSpec((None, d, dff), lambda i: (layer, 0, 0), pipeline_mode=once),
            pl.BlockSpec((None, dff, d), lambda i: (layer, 0, 0), pipeline_mode=once)]),
        out_specs=pl.BlockSpec((tm, d), lambda i: (i, 0)),
        out_shape=jax.ShapeDtypeStruct((t - first_row, d), F32),
        compiler_params=_cparams(("parallel",)),
        name="out_proj_mlp_residuals",
    )(*mix_a, *mix_b, *h, mods, g_post_mix, g_pre_ff, g_post_ff, w_out, w1, w2)


def _rwkv_units(units):
    n = units[0][0].shape[0]
    n2 = 2 * n
    rev = [u[6] for u in units]
    ri = lax.broadcasted_iota(jnp.int32, (n, n), 0)
    ci = lax.broadcasted_iota(jnp.int32, (n, n), 1)
    tri = {False: (ci <= ri).astype(F32), True: (ci >= ri).astype(F32)}
    rr = lax.broadcasted_iota(jnp.int32, (n2, n2), 0)
    cc = lax.broadcasted_iota(jnp.int32, (n2, n2), 1)
    r2, c2 = rr % n, cc % n
    incl2 = {False: c2 <= r2, True: c2 >= r2}
    strict2 = {False: c2 < r2, True: c2 > r2}
    eye = rr == cc
    eye_f = jnp.where(eye, 1.0, 0.0)
    zero = jnp.zeros((n2, n2), F32)
    base = 8
    diag_blk = rr // base == cc // base
    sizes = []
    size = base
    while size < n:
        sizes.append(size)
        size *= 2
    off_blk = [(rr // (2 * s) == cc // (2 * s)) & (rr // s != cc // s) for s in sizes]

    cs = [_dot_exact_lhs(tri[u[6]], u[5], 3) for u in units]
    ops = []
    for (r, v, kd, alpha, beta, logw, _), c in zip(units, cs):
        c_all = jnp.sum(logw, axis=0, keepdims=True)
        e_pos = jnp.exp(c)
        e_neg = jnp.exp(-c)
        e_end = jnp.exp(c_all - c)
        ops.append(dict(rt=_embed(r * e_pos), at=_embed(alpha * jnp.exp(c - logw)), kt=_embed(kd * e_neg),
                        bt=_embed(beta * e_neg), kh=_embed(kd * e_end), bh=_embed(beta * e_end), vb=_embed(v),
                        decay=jnp.exp(c_all)))
    a_all = [_dot_nt(jnp.concatenate([o["rt"], o["at"]], axis=0), jnp.concatenate([o["kt"], o["bt"]], axis=0))
             for o in ops]
    a_qk = [jnp.where(incl2[f], a[:n2, :n2], zero) for a, f in zip(a_all, rev)]
    a_qb = [jnp.where(incl2[f], a[:n2, n2:], zero) for a, f in zip(a_all, rev)]
    a_ak = [jnp.where(strict2[f], a[n2:, :n2], zero) for a, f in zip(a_all, rev)]
    a_ab = [jnp.where(strict2[f], a[n2:, n2:], zero) for a, f in zip(a_all, rev)]
    akv = [_dot(a, o["vb"]) for a, o in zip(a_ak, ops)]
    apow = [jnp.where(diag_blk, a, zero) for a in a_ab]
    inv = [eye_f + a for a in apow]
    d2 = [_dot(a, a) for a in apow]
    both = [_dot(jnp.concatenate([a, x], axis=0), a) for a, x in zip(d2, inv)]
    inv = [x + b[n2:] for x, b in zip(inv, both)]
    inv = [x + _dot(x, b[:n2]) for x, b in zip(inv, both)]
    for blk in off_blk:
        right = [_dot(jnp.where(blk, a, zero), x) for a, x in zip(a_ab, inv)]
        inv = [x + _dot(x, y) for x, y in zip(inv, right)]
    wu = [_dot(x, jnp.concatenate([o["at"], y], axis=1)) for x, o, y in zip(inv, ops, akv)]
    outs = []
    for o, x, qb, qk in zip(ops, wu, a_qb, a_qk):
        w = x[:, :LANES]
        u0v = jnp.concatenate([x[:, LANES:], o["vb"]], axis=0)
        m = jnp.where(eye, o["decay"], 0.0) + _dot_tn(o["bh"], w)
        nn = _dot_tn(jnp.concatenate([o["bh"], o["kh"]], axis=0), u0v)
        rm = o["rt"] + _dot(qb, w)
        y0 = _dot(jnp.concatenate([qb, qk], axis=1), u0v)
        outs.append((_fold(rm), _fold(y0), _fold(m), _fold(nn)))
    return outs


def _shifted(x, prev_row, next_row, mu_prev, mu_next):
    rows = x.shape[0]
    ridx = lax.broadcasted_iota(jnp.int32, x.shape, 0)
    prev = jnp.where(ridx == 0, prev_row, pltpu.roll(x, 1, 0))
    nxt = jnp.where(ridx == rows - 1, next_row, pltpu.roll(x, rows - 1, 0))
    return x + mu_prev * (prev - x) + mu_next * (nxt - x)


PRE_PAIRS = 4


def _rwkv_pre_body(rkv_ref, rkv_p_ref, rkv_n_ref, lo_ref, lo_p_ref, lo_n_ref,
                   mu_rkv_ref, mu_lo_ref, vec_ref, w2_ref, a2_ref, g2_ref,
                   rm_ref, y0_ref, m_ref, n_ref, bv_ref, g_ref,
                   r_s, v_s, kd_s, al_s, be_s, lw_s, *, lat_tiles, ctx_tiles, n_ctx_tiles):
    i = pl.program_id(0)
    lat = i >= n_ctx_tiles
    pos = jnp.where(lat, (i - n_ctx_tiles) % lat_tiles, i % ctx_tiles)
    has_prev = (pos != 0).astype(F32)
    has_next = (pos != jnp.where(lat, lat_tiles, ctx_tiles) - 1).astype(F32)
    lo = _shifted(lo_ref[...], lo_p_ref[7:8, :] * has_prev, lo_n_ref[0:1, :] * has_next,
                  mu_lo_ref[0:1, :], mu_lo_ref[1:2, :])
    wl = jnp.tanh(lo[:, 0:LANES])
    al = lo[:, LANES:2 * LANES]
    gl = _sigmoid(lo[:, 2 * LANES:3 * LANES])
    ones = _head_ones()
    for p in range(PRE_PAIRS):
        rkv = _shifted(rkv_ref[p], rkv_p_ref[p, 7:8, :] * has_prev, rkv_n_ref[p, 0:1, :] * has_next,
                       mu_rkv_ref[p, 0:1, :], mu_rkv_ref[p, 1:2, :])
        r = rkv[:, 0:LANES]
        k = rkv[:, LANES:2 * LANES]
        v = rkv[:, 2 * LANES:3 * LANES]
        kx = k * vec_ref[p, 4:5, :]
        kk = kx / jnp.maximum(jnp.sqrt(_dot_exact_rhs(kx * kx, ones, 2)), 1e-12)
        r_s[p] = r
        v_s[p] = v
        al_s[p] = -kk
        ksum = jnp.zeros_like(k)
        for d in range(2):
            w_raw = vec_ref[p, d:d + 1, :] + _dot(wl, w2_ref[d, p])
            z = -w_raw
            softplus = jnp.maximum(z, 0.0) + jnp.log(1.0 + jnp.exp(-jnp.abs(z)))
            lw_s[p, d] = -jnp.exp(-softplus - 0.5)
            iclr = _sigmoid(vec_ref[p, 2 + d:3 + d, :] + _dot(al, a2_ref[d, p]))
            kd = k * (1.0 + (iclr - 1.0) * vec_ref[p, 5:6, :])
            kd_s[p, d] = kd
            be_s[p, d] = kk * iclr
            ksum = ksum + kd
        bonus = _dot_exact_rhs(r * ksum * vec_ref[p, 6:7, :], ones, 2)
        bv_ref[p] = bonus * v
        g_ref[p] = _dot(gl, g2_ref[p])

    where = [(p, ci, d) for p in range(PRE_PAIRS) for ci in range(RW_TILE // CHUNK) for d in range(2)]
    units = []
    for p, ci, d in where:
        rows = slice(ci * CHUNK, (ci + 1) * CHUNK)
        units.append((r_s[p, rows, :], v_s[p, rows, :], kd_s[p, d, rows, :], al_s[p, rows, :],
                      be_s[p, d, rows, :], lw_s[p, d, rows, :], d == 1))
    outs = _rwkv_units(units)
    for (p, ci, d), (rm, y0, m, nn) in zip(where, outs):
        rows = slice(ci * CHUNK, (ci + 1) * CHUNK)
        rm_ref[d, p, rows, :] = rm.astype(BF16)
        y0_ref[d, p, rows, :] = y0
        m_ref[d, p, rows, :] = m.astype(BF16)
        n_ref[d, p, rows, :] = nn


def _rwkv_precompute(rkv, lo, prm, *, n_ctx_rows, lc, ls):
    _, t, _ = rkv.shape
    n_tiles = t // RW_TILE
    halo = RW_TILE // 8
    last_halo = t // 8 - 1
    pp = PRE_PAIRS
    tile = lambda i, j: (j, i, 0)
    prev = lambda i, j: (j, jnp.maximum(i * halo - 1, 0), 0)
    nxt = lambda i, j: (j, jnp.minimum((i + 1) * halo, last_halo), 0)
    out_dir = jax.ShapeDtypeStruct((2, N_PAIRS, t, LANES), F32)
    out_mat = jax.ShapeDtypeStruct((2, N_PAIRS, t, LANES), BF16)
    out_one = jax.ShapeDtypeStruct((N_PAIRS, t, LANES), F32)
    dir_spec = pl.BlockSpec((2, pp, RW_TILE, LANES), lambda i, j: (0, j, i, 0))
    one_spec = pl.BlockSpec((pp, RW_TILE, LANES), tile)
    body = functools.partial(_rwkv_pre_body, lat_tiles=ls // RW_TILE, ctx_tiles=lc // RW_TILE,
                             n_ctx_tiles=n_ctx_rows // RW_TILE)
    return pl.pallas_call(
        body,
        grid=(n_tiles, N_PAIRS // pp),
        in_specs=[pl.BlockSpec((pp, RW_TILE, 3 * LANES), tile),
                  pl.BlockSpec((pp, 8, 3 * LANES), prev),
                  pl.BlockSpec((pp, 8, 3 * LANES), nxt),
                  pl.BlockSpec((RW_TILE, 3 * LANES), lambda i, j: (i, 0)),
                  pl.BlockSpec((8, 3 * LANES), lambda i, j: (jnp.maximum(i * halo - 1, 0), 0)),
                  pl.BlockSpec((8, 3 * LANES), lambda i, j: (jnp.minimum((i + 1) * halo, last_halo), 0)),
                  pl.BlockSpec((pp, 8, 3 * LANES), lambda i, j: (j, 0, 0)),
                  pl.BlockSpec((8, 3 * LANES), lambda i, j: (0, 0)),
                  pl.BlockSpec((pp, 16, LANES), lambda i, j: (j, 0, 0)),
                  pl.BlockSpec((2, pp, LANES, LANES), lambda i, j: (0, j, 0, 0)),
                  pl.BlockSpec((2, pp, LANES, LANES), lambda i, j: (0, j, 0, 0)),
                  pl.BlockSpec((pp, LANES, LANES), lambda i, j: (j, 0, 0))],
        out_specs=[dir_spec, dir_spec, dir_spec, dir_spec, one_spec, one_spec],
        out_shape=[out_mat, out_dir, out_mat, out_dir, out_one, out_one],
        scratch_shapes=[pltpu.VMEM((pp, RW_TILE, LANES), F32), pltpu.VMEM((pp, RW_TILE, LANES), F32),
                        pltpu.VMEM((pp, 2, RW_TILE, LANES), F32), pltpu.VMEM((pp, RW_TILE, LANES), F32),
                        pltpu.VMEM((pp, 2, RW_TILE, LANES), F32), pltpu.VMEM((pp, 2, RW_TILE, LANES), F32)],
        compiler_params=_cparams(("parallel", "parallel")),
        name="rwkv7_chunk_precompute",
    )(rkv, rkv, rkv, lo, lo, lo, prm["mu_rkv"], prm["mu_lo"], prm["vecs"], prm["w2"], prm["a2"], prm["g2"])


SCAN_PAIRS = 2


def _rwkv_scan_body(rm_c, y0_c, m_c, n_c, rm_l, y0_l, m_l, n_l, bv_c, g_c, bv_l, g_l, vec_ref,
                    oc_ref, ol_ref, yc_s, yl_s):
    chains = [(d, p) for d in range(2) for p in range(SCAN_PAIRS)]

    def run(rm, y0, mm, nn, y_s, states):
        n_chunks = y_s.shape[2] // CHUNK

        def step(t, st):
            out = []
            for (d, p), s in zip(chains, st):
                c = t if d == 0 else n_chunks - 1 - t
                rows = pl.ds(pl.multiple_of(c * CHUNK, CHUNK), CHUNK)
                both = _dot(jnp.concatenate([_embed(rm[d, p, rows, :]), _embed(mm[d, p, rows, :])], axis=0), s)
                y_s[d, p, rows, :] = _fold(both[:LANES]) + y0[d, p, rows, :]
                out.append(both[LANES:] + _embed(nn[d, p, rows, :]))
            return tuple(out)

        return lax.fori_loop(0, n_chunks, step, states)

    zero = jnp.zeros((LANES, LANES), F32)
    states = run(rm_c, y0_c, m_c, n_c, yc_s, tuple(zero for _ in chains))
    run(rm_l, y0_l, m_l, n_l, yl_s, states)

    ones = _head_ones()

    def finish(y_s, bv, g, o_ref):
        def tile(i, carry):
            rows = pl.ds(pl.multiple_of(i * RW_TILE, RW_TILE), RW_TILE)
            for p in range(SCAN_PAIRS):
                y = y_s[0, p, rows, :] + y_s[1, p, rows, :]
                dev = y - _dot_exact_rhs(y, ones, 2) * (1.0 / HEAD_DIM)
                var = _dot_exact_rhs(dev * dev, ones, 2) * (1.0 / HEAD_DIM)
                yn = dev * lax.rsqrt(var + GN_EPS) * vec_ref[p, 7:8, :] + vec_ref[p, 8:9, :]
                o_ref[rows, p * LANES:(p + 1) * LANES] = ((yn + bv[p, rows, :]) * g[p, rows, :]).astype(o_ref.dtype)
            return carry

        lax.fori_loop(0, y_s.shape[2] // RW_TILE, tile, 0)

    finish(yc_s, bv_c, g_c, oc_ref)
    finish(yl_s, bv_l, g_l, ol_ref)


def _rwkv_scan(rm, y0, mm, nn, bv, g, vecs, *, nb, lc, ls):
    pc = SCAN_PAIRS
    ctx_d = pl.BlockSpec((2, pc, lc, LANES), lambda b, j: (0, j, b, 0))
    lat0 = nb * lc // ls
    lat_d = pl.BlockSpec((2, pc, ls, LANES), lambda b, j: (0, j, lat0 + b, 0))
    ctx_1 = pl.BlockSpec((pc, lc, LANES), lambda b, j: (j, b, 0))
    lat_1 = pl.BlockSpec((pc, ls, LANES), lambda b, j: (j, lat0 + b, 0))
    return pl.pallas_call(
        _rwkv_scan_body,
        grid=(nb, N_PAIRS // pc),
        in_specs=[ctx_d, ctx_d, ctx_d, ctx_d, lat_d, lat_d, lat_d, lat_d, ctx_1, ctx_1, lat_1, lat_1,
                  pl.BlockSpec((pc, 16, LANES), lambda b, j: (j, 0, 0))],
        out_specs=[pl.BlockSpec((lc, pc * LANES), lambda b, j: (b, j)),
                   pl.BlockSpec((ls, pc * LANES), lambda b, j: (b, j))],
        out_shape=[jax.ShapeDtypeStruct((nb * lc, GROUP_W), BF16),
                   jax.ShapeDtypeStruct((nb * ls, GROUP_W), BF16)],
        scratch_shapes=[pltpu.VMEM((2, pc, lc, LANES), F32), pltpu.VMEM((2, pc, ls, LANES), F32)],
        compiler_params=_cparams(("parallel", "parallel")),
        name="rwkv7_recurrence_output",
    )(rm, y0, mm, nn, rm, y0, mm, nn, bv, g, bv, g, vecs)


def _gqa_attention(q_pairs, k, v, bias, sinks, stack, together):
    rows = q_pairs[0].shape[0]
    groups = [range(first, first + stack) for first in range(0, 2 * N_PAIRS, stack)]

    def run(batch):
        ss = [_dot_nt(jnp.concatenate([_to_half(q_pairs[h // 2], h % 2, h // N_PAIRS) for h in heads], axis=0), k)
              for heads in batch]
        if bias is not None:
            tiled = jnp.tile(bias, (stack, 1))
            ss = [s + tiled for s in ss]
        ms = [jnp.max(s, axis=-1, keepdims=True) for s in ss]
        sk = None
        if sinks is not None:
            sk = [jnp.concatenate([jnp.broadcast_to(sinks[h], (rows, 1)) for h in heads], axis=0) for heads in batch]
            ms = [jnp.maximum(m, x) for m, x in zip(ms, sk)]
        ps = [jnp.exp2(s - m) for s, m in zip(ss, ms)]
        dens = [jnp.sum(p, axis=-1, keepdims=True) for p in ps]
        if sk is not None:
            dens = [d + jnp.exp2(x - m) for d, x, m in zip(dens, sk, ms)]
        return [_dot(p, v) / d for p, d in zip(ps, dens)]

    stacks = run(groups) if together else [o for grp in groups for o in run([grp])]
    outs = [o[i * rows:(i + 1) * rows] for o in stacks for i in range(stack)]
    return [_merge_heads(outs[2 * p], p // 2, outs[2 * p + 1], p // 2) for p in range(N_PAIRS)]


WINDOW_QB = 4


def _window_body(q_ref, kx_ref, kl_ref, vx_ref, vl_ref, cq_ref, sq_ref, cos_ref, sin_ref, sink_ref,
                 o_ref, k_scr, v_scr, *, lc, ls):
    @pl.when(pl.program_id(1) == 0)
    def _():
        k_scr[0:lc, :] = kx_ref[...].astype(BF16)
        k_scr[lc:, :] = _rope(kl_ref[...], cos_ref[...], sin_ref[...]).astype(BF16)
        v_scr[0:lc, :] = vx_ref[...].astype(BF16)
        v_scr[lc:, :] = vl_ref[...].astype(BF16)

    span = 3 * BLOCK
    nk = lc + span
    qi = lax.broadcasted_iota(jnp.int32, (BLOCK, nk), 0)
    kj = lax.broadcasted_iota(jnp.int32, (BLOCK, nk), 1) - lc
    sinks = [sink_ref[h:h + 1, 0:1] for h in range(2 * N_PAIRS)]
    for i in range(WINDOW_QB):
        rows = slice(i * BLOCK, (i + 1) * BLOCK)
        q0 = (pl.program_id(1) * WINDOW_QB + i) * BLOCK
        k0 = pl.multiple_of(jnp.clip(q0 - BLOCK, 0, ls - span), BLOCK)
        k = jnp.concatenate([k_scr[0:lc, :], k_scr[pl.ds(lc + k0, span), :]], axis=0)
        v = jnp.concatenate([v_scr[0:lc, :], v_scr[pl.ds(lc + k0, span), :]], axis=0)
        ok = (kj < 0) | (jnp.abs((k0 + kj) - (q0 + qi)) <= WINDOW)
        bias = jnp.where(ok, 0.0, NEG_INF)
        q = q_ref[rows, :]
        qp = [_rope(q[:, p * LANES:(p + 1) * LANES], cq_ref[rows, :], sq_ref[rows, :]) * Q_SCALE
              for p in range(N_PAIRS)]
        outs = _gqa_attention(qp, k, v, bias, sinks, stack=4, together=False)
        for p in range(N_PAIRS):
            o_ref[rows, p * LANES:(p + 1) * LANES] = outs[p].astype(o_ref.dtype)


def _window_attention(qkv, cos, sin, sink, *, nb, lc, ls):
    tq = WINDOW_QB * BLOCK
    n_steps = ls // tq
    base = nb * lc // tq
    lat0 = nb * lc // ls
    kcol, vcol = GROUP_W // LANES, GROUP_W // LANES + 1
    return pl.pallas_call(
        functools.partial(_window_body, lc=lc, ls=ls),
        grid=(nb, n_steps),
        in_specs=[pl.BlockSpec((tq, GROUP_W), lambda b, n: (base + b * n_steps + n, 0)),
                  pl.BlockSpec((lc, LANES), lambda b, n: (b, kcol)),
                  pl.BlockSpec((ls, LANES), lambda b, n: (lat0 + b, kcol)),
                  pl.BlockSpec((lc, LANES), lambda b, n: (b, vcol)),
                  pl.BlockSpec((ls, LANES), lambda b, n: (lat0 + b, vcol)),
                  pl.BlockSpec((tq, LANES), lambda b, n: (n, 0)),
                  pl.BlockSpec((tq, LANES), lambda b, n: (n, 0)),
                  pl.BlockSpec((ls, LANES), lambda b, n: (0, 0)),
                  pl.BlockSpec((ls, LANES), lambda b, n: (0, 0)),
                  pl.BlockSpec((8, LANES), lambda b, n: (0, 0))],
        out_specs=pl.BlockSpec((tq, GROUP_W), lambda b, n: (b * n_steps + n, 0)),
        out_shape=jax.ShapeDtypeStruct((nb * ls, GROUP_W), BF16),
        scratch_shapes=[pltpu.VMEM((lc + ls, LANES), BF16), pltpu.VMEM((lc + ls, LANES), BF16)],
        compiler_params=_cparams(("parallel", "arbitrary")),
        name="window_attention",
    )(qkv, qkv, qkv, qkv, qkv, cos, sin, cos, sin, sink)


def _global_body(q_ref, kx_ref, kl_ref, vx_ref, vl_ref, cq_ref, sq_ref, cos_ref, sin_ref, gq_ref, gk_ref,
                 o_ref, k_scr, v_scr, *, lc):
    @pl.when(pl.program_id(1) == 0)
    def _():
        k_scr[0:lc, :] = _head_rms(kx_ref[...], gk_ref[...]).astype(BF16)
        k_scr[lc:, :] = _rope(_head_rms(kl_ref[...], gk_ref[...]), cos_ref[...], sin_ref[...]).astype(BF16)
        v_scr[0:lc, :] = vx_ref[...].astype(BF16)
        v_scr[lc:, :] = vl_ref[...].astype(BF16)

    k = k_scr[...]
    v = v_scr[...]
    for i in range(GLOBAL_QB):
        rows = slice(i * BLOCK, (i + 1) * BLOCK)
        q = q_ref[rows, :]
        qp = [_rope(_head_rms(q[:, p * LANES:(p + 1) * LANES], gq_ref[...]), cq_ref[rows, :], sq_ref[rows, :])
              * Q_SCALE for p in range(N_PAIRS)]
        outs = _gqa_attention(qp, k, v, None, None, stack=2, together=True)
        for p in range(N_PAIRS):
            o_ref[rows, p * LANES:(p + 1) * LANES] = outs[p].astype(o_ref.dtype)


GLOBAL_QB = 4


def _global_attention(qkv, cos, sin, q_gain, k_gain, *, nb, lc, ls):
    tq = GLOBAL_QB * BLOCK
    n_blocks = ls // tq
    base = nb * lc // tq
    lat0 = nb * lc // ls
    kcol, vcol = GROUP_W // LANES, GROUP_W // LANES + 1
    return pl.pallas_call(
        functools.partial(_global_body, lc=lc),
        grid=(nb, n_blocks),
        in_specs=[pl.BlockSpec((tq, GROUP_W), lambda b, n: (base + b * n_blocks + n, 0)),
                  pl.BlockSpec((lc, LANES), lambda b, n: (b, kcol)),
                  pl.BlockSpec((ls, LANES), lambda b, n: (lat0 + b, kcol)),
                  pl.BlockSpec((lc, LANES), lambda b, n: (b, vcol)),
                  pl.BlockSpec((ls, LANES), lambda b, n: (lat0 + b, vcol)),
                  pl.BlockSpec((tq, LANES), lambda b, n: (n, 0)),
                  pl.BlockSpec((tq, LANES), lambda b, n: (n, 0)),
                  pl.BlockSpec((ls, LANES), lambda b, n: (0, 0)),
                  pl.BlockSpec((ls, LANES), lambda b, n: (0, 0)),
                  pl.BlockSpec((1, LANES), lambda b, n: (0, 0)),
                  pl.BlockSpec((1, LANES), lambda b, n: (0, 0))],
        out_specs=pl.BlockSpec((tq, GROUP_W), lambda b, n: (b * n_blocks + n, 0)),
        out_shape=jax.ShapeDtypeStruct((nb * ls, GROUP_W), BF16),
        scratch_shapes=[pltpu.VMEM((lc + ls, LANES), BF16), pltpu.VMEM((lc + ls, LANES), BF16)],
        compiler_params=_cparams(("parallel", "arbitrary")),
        name="global_attention",
    )(qkv, qkv, qkv, qkv, qkv, cos, sin, cos, sin, q_gain, k_gain)


NA_ROWS = 4


def _na_body(q_ref, kx_ref, kl_ref, vx_ref, vl_ref, bias_ref, o_ref, *, n_rows, lc):
    kh = min(NA_KH, n_rows)
    first = _lane_half((GRID_W, LANES))
    probs = []
    for i in range(NA_ROWS):
        r = pl.program_id(1) * NA_ROWS + i
        rs = jnp.clip(r - kh // 2, 0, n_rows - kh)
        win = pl.ds(pl.multiple_of(rs * GRID_W, GRID_W), kh * GRID_W)
        for p in range(N_PAIRS):
            probs.append((slice(i * GRID_W, (i + 1) * GRID_W), slice(p * LANES, (p + 1) * LANES), win, r - rs, p))
    pad = jnp.zeros((2 * GRID_W, lc), F32)
    ks = [jnp.concatenate([kx_ref[:, cols], kl_ref[win, cols]], axis=0).astype(BF16) for _, cols, win, _, _ in probs]
    vs = [jnp.concatenate([vx_ref[:, cols], vl_ref[win, cols]], axis=0).astype(BF16) for _, cols, win, _, _ in probs]
    ss = [_dot_nt(_embed(q_ref[rows, cols] * Q_SCALE), k) + jnp.concatenate([pad, bias_ref[var, p]], axis=1)
          for (rows, cols, _, var, p), k in zip(probs, ks)]
    ms = [jnp.max(s, axis=-1, keepdims=True) for s in ss]
    ps = [jnp.exp2(s - m) for s, m in zip(ss, ms)]
    dens = [jnp.sum(p, axis=-1, keepdims=True) for p in ps]
    outs = [_dot(p, v) / den for p, v, den in zip(ps, vs, dens)]
    for (rows, cols, _, _, _), o in zip(probs, outs):
        o_ref[rows, cols] = jnp.where(first, o[:GRID_W], o[GRID_W:]).astype(o_ref.dtype)


def _na_bias_table(rpb, n_rows):
    kh = min(NA_KH, n_rows)
    qc = np.arange(GRID_W)[:, None]
    kc = np.arange(GRID_W)[None, :]
    win_start = np.clip(qc - NA_KW // 2, 0, GRID_W - NA_KW)
    col_ok = (kc >= win_start) & (kc < win_start + NA_KW)
    dc = np.clip(kc - qc + NA_KW - 1, 0, 2 * NA_KW - 2)
    n_dc = 2 * NA_KW - 1
    heads, n_dr = rpb.shape[0], rpb.shape[1]
    onehot = (dc.reshape(1, -1) == np.arange(n_dc)[:, None]).astype(np.float32)
    by_col = jnp.dot(rpb.reshape(heads * n_dr, n_dc), onehot, precision=lax.Precision.HIGHEST)
    by_col = jnp.where(col_ok[None, None], by_col.reshape(heads, n_dr, GRID_W, GRID_W), NEG_INF)
    tab = jnp.stack([by_col[:, NA_KH - 1 - var:NA_KH - 1 - var + kh] for var in range(kh)])
    tab = tab.transpose(0, 1, 3, 2, 4).reshape(kh, N_PAIRS, 2 * GRID_W, kh * GRID_W)
    return (tab * LOG2E).astype(F32)


def _neighbourhood_attention(qkv, bias, *, nb, lc, ls):
    n_rows = ls // GRID_W
    kh = min(NA_KH, n_rows)
    lat0 = nb * lc // ls
    assert n_rows % NA_ROWS == 0
    steps = n_rows // NA_ROWS
    tq = NA_ROWS * GRID_W
    base = nb * lc // tq
    return pl.pallas_call(
        functools.partial(_na_body, n_rows=n_rows, lc=lc),
        grid=(nb, steps),
        in_specs=[pl.BlockSpec((tq, GROUP_W), lambda b, r: (base + b * steps + r, 0)),
                  pl.BlockSpec((lc, GROUP_W), lambda b, r: (b, 1)),
                  pl.BlockSpec((ls, GROUP_W), lambda b, r: (lat0 + b, 1)),
                  pl.BlockSpec((lc, GROUP_W), lambda b, r: (b, 2)),
                  pl.BlockSpec((ls, GROUP_W), lambda b, r: (lat0 + b, 2)),
                  pl.BlockSpec((kh, N_PAIRS, 2 * GRID_W, kh * GRID_W), lambda b, r: (0, 0, 0, 0),
                               pipeline_mode=pl.Buffered(1))],
        out_specs=pl.BlockSpec((tq, GROUP_W), lambda b, r: (b * steps + r, 0)),
        out_shape=jax.ShapeDtypeStruct((nb * ls, GROUP_W), BF16),
        compiler_params=_cparams(("parallel", "parallel")),
        name="neighbourhood_attention",
    )(qkv, qkv, qkv, qkv, qkv, bias)


def _ctx_body(q_ref, k_ref, v_ref, sink_ref, gq_ref, gk_ref, o_ref, *, gqa, use_sink, use_norm):
    q = q_ref[...]
    k = k_ref[...]
    v = v_ref[...].astype(BF16)
    if gqa:
        if use_norm:
            k = _head_rms(k, gk_ref[...])
        k = k.astype(BF16)
        qp = []
        for p in range(N_PAIRS):
            x = q[:, p * LANES:(p + 1) * LANES]
            if use_norm:
                x = _head_rms(x, gq_ref[...])
            qp.append(x * Q_SCALE)
        sinks = [sink_ref[h:h + 1, 0:1] for h in range(2 * N_PAIRS)] if use_sink else None
        outs = _gqa_attention(qp, k, v, None, sinks, stack=4, together=False)
        for p in range(N_PAIRS):
            o_ref[:, p * LANES:(p + 1) * LANES] = outs[p].astype(o_ref.dtype)
    else:
        rows = q.shape[0]
        for p in range(N_PAIRS):
            cols = slice(p * LANES, (p + 1) * LANES)
            o = _softmax_pv(_dot_nt(_embed(q[:, cols] * Q_SCALE), k[:, cols]), v[:, cols])
            o_ref[:, cols] = jnp.where(_lane_half((rows, LANES)), o[:rows], o[rows:]).astype(o_ref.dtype)


def _ctx_attention(qkv, *, nb, lc, gqa, q_col=0, sink=None, q_gain=None, k_gain=None):
    kw = KV_W if gqa else GROUP_W
    kcol = (q_col * GROUP_W + GROUP_W) // kw
    dummy = jnp.zeros((8, LANES), F32)
    one = jnp.ones((1, LANES), F32)
    return pl.pallas_call(
        functools.partial(_ctx_body, gqa=gqa, use_sink=sink is not None, use_norm=q_gain is not None),
        grid=(nb,),
        in_specs=[pl.BlockSpec((lc, GROUP_W), lambda b: (b, q_col)),
                  pl.BlockSpec((lc, kw), lambda b: (b, kcol)),
                  pl.BlockSpec((lc, kw), lambda b: (b, kcol + 1)),
                  pl.BlockSpec((8, LANES), lambda b: (0, 0)),
                  pl.BlockSpec((1, LANES), lambda b: (0, 0)),
                  pl.BlockSpec((1, LANES), lambda b: (0, 0))],
        out_specs=pl.BlockSpec((lc, GROUP_W), lambda b: (b, 0)),
        out_shape=jax.ShapeDtypeStruct((nb * lc, GROUP_W), BF16),
        compiler_params=_cparams(("parallel",)),
        name="context_attention",
    )(qkv, qkv, qkv, dummy if sink is None else sink, one if q_gain is None else q_gain,
      one if k_gain is None else k_gain)


def _rope_tables(ls):
    t = jnp.arange(ls, dtype=jnp.int32)
    n_freq = HEAD_DIM // 4
    inv_freq = ROPE_BASE ** (-jnp.arange(n_freq, dtype=F32) / n_freq)
    row = (t // GRID_W).astype(F32)[:, None] * inv_freq
    col = (t % GRID_W).astype(F32)[:, None] * inv_freq
    cos = jnp.concatenate([jnp.cos(row), jnp.cos(row), jnp.cos(col), jnp.cos(col)], axis=-1)
    sin = jnp.concatenate([-jnp.sin(row), jnp.sin(row), -jnp.sin(col), jnp.sin(col)], axis=-1)
    return jnp.tile(cos, (1, 2)), jnp.tile(sin, (1, 2))


def _rwkv_params(mu_prev, mu_next, w0, w2, a0, a2, g2, k_k, k_a, r_k, gn_w, gn_b):
    aw = GROUP_W
    pair = lambda x: x.reshape(N_PAIRS, LANES)

    def mu_rkv(mu):
        return jnp.concatenate([pair(mu[0:aw]), pair(mu[aw:2 * aw]), pair(mu[2 * aw:3 * aw])], axis=-1)

    mu_a = jnp.zeros((N_PAIRS, 8, 3 * LANES), F32)
    mu_a = mu_a.at[:, 0].set(mu_rkv(mu_prev)).at[:, 1].set(mu_rkv(mu_next))
    mu_l = jnp.zeros((8, 3 * LANES), F32).at[0].set(mu_prev[3 * aw:]).at[1].set(mu_next[3 * aw:])
    vecs = jnp.zeros((N_PAIRS, 16, LANES), F32)
    rows = [pair(w0[0]), pair(w0[1]), pair(a0[0]), pair(a0[1]), pair(k_k), pair(k_a), pair(r_k.reshape(-1)),
            pair(gn_w), pair(gn_b)]
    for i, val in enumerate(rows):
        vecs = vecs.at[:, i].set(val)

    def lora_pad(w):
        out = jnp.zeros((2, N_PAIRS, LANES, LANES), F32)
        for d in range(2):
            blk = w[d].reshape(DECAY_LORA, N_PAIRS, LANES).transpose(1, 0, 2)
            out = out.at[d, :, d * DECAY_LORA:(d + 1) * DECAY_LORA, :].set(blk)
        return out

    g2p = g2.reshape(GATE_LORA, N_PAIRS, LANES).transpose(1, 0, 2)
    return dict(mu_rkv=mu_a, mu_lo=mu_l, vecs=vecs, w2=lora_pad(w2), a2=lora_pad(a2), g2=g2p)


def _lane_bcast(x, rows):
    return jnp.zeros((rows, LANES), F32).at[:x.shape[0]].set(jnp.broadcast_to(x[:, None], (x.shape[0], LANES)))


A_W = GROUP_W
A_IN = 3 * A_W + 3 * LANES
B_IN = GROUP_W + 2 * KV_W


def _even_mixer(h, mods, gains, w_in, layer, w_layer, a_prm, sink, cos, sin, *, dims):
    nb, lc, ls = dims["nb"], dims["lc"], dims["ls"]
    t = nb * (lc + ls)
    rkv_dests = [(0, j, part * LANES, part * A_W + j * LANES, LANES)
                 for j in range(N_PAIRS) for part in range(3)]
    pieces = [(0, 3 * A_W, rkv_dests),
              (3 * A_W, 3 * LANES, [(1, None, 0, 0, 3 * LANES)]),
              (A_IN, B_IN, [(2, None, 0, 0, B_IN)])]
    rkv, lo, qkv = _in_proj(h, mods, gains, w_in, layer, w_layer,
                            [(N_PAIRS, t, 3 * LANES), (t, 3 * LANES), (t, B_IN)], pieces,
                            tm=dims["tm"], n_ctx_rows=nb * lc, ls=ls, nb=nb)
    rm, y0, mm, nn, bv, g = _rwkv_precompute(rkv, lo, a_prm, n_ctx_rows=nb * lc, lc=lc, ls=ls)
    a_ctx, a_lat = _rwkv_scan(rm, y0, mm, nn, bv, g, a_prm["vecs"], nb=nb, lc=lc, ls=ls)
    sink_b = _lane_bcast(sink * LOG2E, 8)
    b_lat = _window_attention(qkv, cos, sin, sink_b, nb=nb, lc=lc, ls=ls)
    b_ctx = _ctx_attention(qkv, nb=nb, lc=lc, gqa=True, sink=sink_b)
    return (a_ctx, a_lat), (b_ctx, b_lat)


def _odd_mixer(h, mods, gains, w_in, layer, w_layer, rpb, q_gain, k_gain, cos, sin, *, dims):
    nb, lc, ls = dims["nb"], dims["lc"], dims["ls"]
    t = nb * (lc + ls)
    c_in = 3 * GROUP_W
    pieces = [(0, c_in, [(0, None, 0, 0, c_in)]), (c_in, B_IN, [(1, None, 0, 0, B_IN)])]
    qkv_c, qkv_d = _in_proj(h, mods, gains, w_in, layer, w_layer, [(t, c_in), (t, B_IN)], pieces,
                            tm=dims["tm"], n_ctx_rows=nb * lc, ls=ls, nb=nb)
    bias = _na_bias_table(rpb, ls // GRID_W)
    c_lat = _neighbourhood_attention(qkv_c, bias, nb=nb, lc=lc, ls=ls)
    c_ctx = _ctx_attention(qkv_c, nb=nb, lc=lc, gqa=False)
    gq = jnp.tile(q_gain, 2)[None, :]
    gk = jnp.tile(k_gain, 2)[None, :]
    d_lat = _global_attention(qkv_d, cos, sin, gq, gk, nb=nb, lc=lc, ls=ls)
    d_ctx = _ctx_attention(qkv_d, nb=nb, lc=lc, gqa=True, q_gain=gq, k_gain=gk)
    return (c_ctx, c_lat), (d_ctx, d_lat)


def _row_tile(target, rows_ctx, ls):
    tile = target
    while rows_ctx % tile or ls % tile:
        tile //= 2
    return tile


def kernel(x, c, ctx, c_ctx, w_ada, b_ada, g_pre_mix, g_post_mix, g_pre_ff, g_post_ff, w_in_even, w_in_odd, w_out, w_ff1, w_ff2, a_mu_prev, a_mu_next, a_w0, a_w2, a_a0, a_a2, a_g2, a_k_k, a_k_a, a_r_k, a_gn_w, a_gn_b, b_sink, c_rpb, d_q_gain, d_k_gain):
    nb, ls, d = x.shape
    lc = ctx.shape[1]
    depth = w_ada.shape[0]
    assert nb < 16 and lc % RW_TILE == 0 and ls % RW_TILE == 0 and (nb * lc) % ls == 0
    n_ctx_rows = nb * lc
    tm = _row_tile(512, n_ctx_rows, ls)
    dims = dict(nb=nb, lc=lc, ls=ls, tm=tm)
    cvec = jnp.zeros((16, d), F32).at[:nb].set(c).at[nb].set(c_ctx)
    mods = _modulation(cvec, w_ada, b_ada)
    cos, sin = _rope_tables(ls)
    stack = lambda g: g.reshape(depth, 1, d)
    g_pre_mix, g_post_mix, g_pre_ff, g_post_ff = map(stack, (g_pre_mix, g_post_mix, g_pre_ff, g_post_ff))
    w_in_even, w_in_odd, w_out, w_ff1, w_ff2 = (w.astype(BF16) for w in (w_in_even, w_in_odd, w_out, w_ff1, w_ff2))
    h = (ctx.reshape(n_ctx_rows, d), x.reshape(nb * ls, d))
    for i in range(depth):
        j = i // 2
        last = i == depth - 1
        if i % 2 == 0:
            a_prm = _rwkv_params(a_mu_prev[j], a_mu_next[j], a_w0[j], a_w2[j], a_a0[j], a_a2[j], a_g2[j],
                                 a_k_k[j], a_k_a[j], a_r_k[j], a_gn_w[j], a_gn_b[j])
            mix_a, mix_b = _even_mixer(h, mods, g_pre_mix, w_in_even, i, j, a_prm, b_sink[j], cos, sin, dims=dims)
        else:
            mix_a, mix_b = _odd_mixer(h, mods, g_pre_mix, w_in_odd, i, j, c_rpb[j], d_q_gain[j], d_k_gain[j],
                                      cos, sin, dims=dims)
        h = (_mix_mlp(mix_a, mix_b, h, mods, g_post_mix, g_pre_ff, g_post_ff, w_out, w_ff1, w_ff2, i,
                      tm=tm, n_split=2, n_ctx_rows=n_ctx_rows, ls=ls, nb=nb, first_row=n_ctx_rows if last else 0),)
    return h[0].reshape(nb, ls, d)
```
